```python
import jax, jax.numpy as jnp
from jax import lax
import numpy as np

D_MODEL = 1024
BATCH = 8
SEQ = 4096
DEPTH = 1

HG_HEADS = 8
HG_DK = 128
HG_DV = D_MODEL // HG_HEADS
HG_CHUNK = 64
RET_HEADS = 8
RET_DV = D_MODEL // RET_HEADS
RET_DK = RET_DV // 2
RET_CHUNK = 128
ROPE_BASE = 10000.0
EPS = 1e-6

HG_Q = HG_HEADS * HG_DK
HG_V = HG_HEADS * HG_DV
RET_Q = RET_HEADS * RET_DK
RET_V = RET_HEADS * RET_DV
SPLITS = [HG_Q, HG_Q, HG_V, HG_V, RET_Q, RET_Q, RET_V, RET_V, D_MODEL, D_MODEL]
D_IN = sum(SPLITS)

kernel_name = "hybrid_hgrn2_retention_gated_block"


def rms_norm(x, g):
    xf = x.astype(jnp.float32)
    y = xf * lax.rsqrt(jnp.mean(xf * xf, axis=-1, keepdims=True) + EPS)
    return (y * g.astype(jnp.float32)).astype(x.dtype)


def head_rms_norm(o, g):
    H, d = o.shape[-2], o.shape[-1]
    y = o * lax.rsqrt(jnp.mean(o * o, axis=-1, keepdims=True) + EPS)
    y = y * g.astype(jnp.float32).reshape(H, d)
    return y.reshape(o.shape[0], o.shape[1], H * d)


def rotary(t, pos):
    dk = t.shape[-1]
    inv_freq = 1.0 / (ROPE_BASE ** jnp.linspace(0.0, 1.0, dk // 2, dtype=jnp.float32))
    ang = pos.astype(jnp.float32)[:, None] * inv_freq[None, :]
    cos = jnp.cos(ang)[None, :, None, :]
    sin = jnp.sin(ang)[None, :, None, :]
    t1, t2 = t[..., : dk // 2], t[..., dk // 2:]
    return jnp.concatenate([t1 * cos - t2 * sin, t1 * sin + t2 * cos], axis=-1)


def hgrn2_chunkwise(q, fpre, v, lb):
    B, L, H, dk = q.shape
    dv = v.shape[-1]
    C = HG_CHUNK
    N = L // C
    q = jax.nn.silu(q)
    lbh = lb.astype(jnp.float32).reshape(H, dk)
    f = lbh + (1.0 - lbh) * jax.nn.sigmoid(fpre)
    k = 1.0 - f
    logf = jnp.log(f)

    def to_chunks(t):
        return t.reshape(B, N, C, H, t.shape[-1]).transpose(1, 0, 3, 2, 4)

    mask = jnp.tril(jnp.ones((C, C), dtype=bool))[:, :, None]

    def step(S, inp):
        qc, kc, lfc, vc = inp
        b = jnp.cumsum(lfc, axis=2)
        inter = jnp.einsum('bhtd,bhde->bhte', qc * jnp.exp(b), S)
        diff = b[:, :, :, None, :] - b[:, :, None, :, :]
        decay = jnp.exp(jnp.where(mask, diff, -jnp.inf))
        A = jnp.einsum('bhtsd,bhsd->bhts', qc[:, :, :, None, :] * decay, kc)
        intra = jnp.einsum('bhts,bhse->bhte', A, vc)
        b_last = b[:, :, -1:, :]
        S_new = jnp.exp(b_last[:, :, 0, :])[..., None] * S + jnp.einsum(
            'bhsd,bhse->bhde', kc * jnp.exp(b_last - b), vc)
        return S_new, inter + intra

    S0 = jnp.zeros((B, H, dk, dv), jnp.float32)
    _, o = lax.scan(step, S0, (to_chunks(q), to_chunks(k), to_chunks(logf), to_chunks(v)))
    return o.transpose(1, 0, 3, 2, 4).reshape(B, L, H, dv)


def retention_chunkwise(q, k, v):
    B, L, H, dk = q.shape
    dv = v.shape[-1]
    C = RET_CHUNK
    N = L // C
    log_gamma = jnp.log(1.0 - jnp.exp2(-5.0 - jnp.arange(H, dtype=jnp.float32)))
    k = k * (dk ** -0.5)
    qc = q.reshape(B, N, C, H, dk).transpose(0, 3, 1, 2, 4)
    kc = k.reshape(B, N, C, H, dk).transpose(0, 3, 1, 2, 4)
    vc = v.reshape(B, N, C, H, dv).transpose(0, 3, 1, 2, 4)
    idx = jnp.arange(C, dtype=jnp.float32)
    rel = idx[:, None] - idx[None, :]
    Dm = jnp.where(rel >= 0, jnp.exp(log_gamma[:, None, None] * jnp.maximum(rel, 0.0)), 0.0)
    scores = jnp.einsum('bhnid,bhnjd->bhnij', qc, kc) * Dm[:, None]
    intra = jnp.einsum('bhnij,bhnje->bhnie', scores, vc)
    zeta = jnp.exp(log_gamma[:, None] * (C - 1.0 - idx))
    chunk_state = jnp.einsum('bhnjd,bhnje->bhnde', kc * zeta[:, None, :, None], vc)
    chunk_decay = jnp.exp(log_gamma * C)[None, :, None, None]

    def step(R, s):
        return chunk_decay * R + s, R

    R0 = jnp.zeros((B, H, dk, dv), jnp.float32)
    _, R_prev = lax.scan(step, R0, chunk_state.transpose(2, 0, 1, 3, 4))
    R_prev = R_prev.transpose(1, 2, 0, 3, 4)
    xi = jnp.exp(log_gamma[:, None] * (idx + 1.0))
    cross = jnp.einsum('bhnid,bhnde->bhnie', qc * xi[:, None, :, None], R_prev)
    return (intra + cross).transpose(0, 2, 3, 1, 4).reshape(B, L, H, dv)


def setup_inputs(seed: int = 0) -> dict:
    key = jax.random.key(seed)
    ks = jax.random.split(key, 12)
    f32 = jnp.float32
    x = jax.random.normal(ks[0], (BATCH, SEQ, D_MODEL), f32)
    c = jax.random.normal(ks[1], (BATCH, D_MODEL), f32)
    norm_g = 1.0 + 0.05 * jax.random.normal(ks[2], (DEPTH, D_MODEL), f32)
    w_ada = 0.5 * D_MODEL ** -0.5 * jax.random.normal(ks[3], (DEPTH, D_MODEL, 3 * D_MODEL), f32)
    b_ada = 0.02 * jax.random.normal(ks[4], (DEPTH, 3 * D_MODEL), f32)
    w_in = D_MODEL ** -0.5 * jax.random.normal(ks[5], (DEPTH, D_MODEL, D_IN), f32)
    hg_lb_logits = 0.5 * jax.random.normal(ks[6], (DEPTH + 1, HG_Q), f32)
    hg_norm_g = 1.0 + 0.05 * jax.random.normal(ks[7], (DEPTH, HG_V), f32)
    ret_norm_g = 1.0 + 0.05 * jax.random.normal(ks[8], (DEPTH, RET_V), f32)
    w_out = D_MODEL ** -0.5 * jax.random.normal(ks[9], (DEPTH, D_MODEL, D_MODEL), f32)
    final_g = 1.0 + 0.05 * jax.random.normal(ks[10], (D_MODEL,), f32)
    return {"x": x, "c": c, "norm_g": norm_g, "w_ada": w_ada, "b_ada": b_ada,
            "w_in": w_in, "hg_lb_logits": hg_lb_logits, "hg_norm_g": hg_norm_g,
            "ret_norm_g": ret_norm_g, "w_out": w_out, "final_g": final_g}


def reference(x, c, norm_g, w_ada, b_ada, w_in, hg_lb_logits, hg_norm_g, ret_norm_g, w_out, final_g):
    B, L, D = x.shape
    pos = jnp.arange(L, dtype=jnp.int32)
    lower_bounds = jnp.cumsum(jax.nn.softmax(hg_lb_logits.astype(jnp.float32), axis=0), axis=0)
    offsets = np.cumsum([0] + SPLITS)[1:-1].tolist()
    for layer in range(DEPTH):
        mod = jax.nn.silu(c) @ w_ada[layer] + b_ada[layer]
        shift, scale, gate = jnp.split(mod[:, None, :], 3, axis=-1)
        h = rms_norm(x, norm_g[layer]) * (1.0 + scale) + shift
        proj = h @ w_in[layer]
        (hq, hf, hi, hz, rq, rk, rv, rz, ga, gb) = jnp.split(proj, offsets, axis=-1)
        f32 = jnp.float32
        oA = hgrn2_chunkwise(hq.astype(f32).reshape(B, L, HG_HEADS, HG_DK),
                             hf.astype(f32).reshape(B, L, HG_HEADS, HG_DK),
                             hi.astype(f32).reshape(B, L, HG_HEADS, HG_DV),
                             lower_bounds[layer])
        uA = head_rms_norm(oA, hg_norm_g[layer]) * jax.nn.silu(hz.astype(f32))
        qB = rotary(rq.astype(f32).reshape(B, L, RET_HEADS, RET_DK), pos)
        kB = rotary(rk.astype(f32).reshape(B, L, RET_HEADS, RET_DK), pos)
        oB = retention_chunkwise(qB, kB, rv.astype(f32).reshape(B, L, RET_HEADS, RET_DV))
        uB = head_rms_norm(oB, ret_norm_g[layer]) * jax.nn.silu(rz.astype(f32))
        m = (jax.nn.sigmoid(ga.astype(f32)) * uA + jax.nn.sigmoid(gb.astype(f32)) * uB).astype(x.dtype)
        x = x + gate * (m @ w_out[layer])
    return rms_norm(x, final_g)
```

```python
import functools

import jax
import jax.numpy as jnp
from jax import lax
from jax.experimental import pallas as pl
from jax.experimental.pallas import tpu as pltpu

F32 = jnp.float32
BF16 = jnp.bfloat16

D_MODEL = 1024
N_HEADS = 8
HEAD_DV = 128
HG_DK = 128
RET_DK = 64
ROPE_BASE = 10000.0
EPS = 1e-6

TILE_T = 256
HG_CHUNK = 64
RET_CHUNK = 128
N_PAIRS = N_HEADS // 2
SUBLANES = 8

OFF_HQ, OFF_HF, OFF_HI, OFF_HZ = 0, 1024, 2048, 3072
OFF_RQ, OFF_RK, OFF_RV, OFF_RZ = 4096, 4608, 5120, 6144
OFF_GA, OFF_GB = 7168, 8192
D_IN = 9216
PROJ_BLOCK = 1024

EXP_CLAMP = 60.0

V7X_VMEM_LIMIT_BYTES = 58 * 1024 * 1024

NT_DIMS = (((1,), (1,)), ((), ()))
TN_DIMS = (((0,), (0,)), ((), ()))


def _sigmoid(x):
    return 0.5 * (jnp.tanh(0.5 * x) + 1.0)


def _silu(x):
    return x * _sigmoid(x)


def _dot(a, b):
    return jnp.dot(a, b, preferred_element_type=F32)


def _prep_kernel(c_ref, wada_ref, bada_ref, lbl_ref, mod_ref, lb_ref):
    c = c_ref[...]
    sc = _silu(c).astype(BF16)
    mod_ref[...] = _dot(sc, wada_ref[...].astype(BF16)) + bada_ref[...]
    lg = lbl_ref[...]
    l0, l1 = lg[0:1, :], lg[1:2, :]
    mx = jnp.maximum(l0, l1)
    e0, e1 = jnp.exp(l0 - mx), jnp.exp(l1 - mx)
    lb_ref[...] = e0 / (e0 + e1)


def _layer_kernel(x_ref, mod_ref, ng_ref, win_ref, lb_ref, hgg_ref, rtg_ref, fg_ref, wout_ref,
                  cos_ref, sina_ref, sinb_ref, dm_ref, xi_ref, zeta_ref, dec_ref, ltri_ref,
                  o_ref,
                  p_ref, k_ref, b_ref, lhi_ref, llo_ref, a_ref, s_ref, r_ref, oa_ref, ob_ref, m_ref):
    T = x_ref.shape[0]
    n_hc = T // HG_CHUNK
    n_rc = T // RET_CHUNK

    @pl.when(pl.program_id(1) == 0)
    def _():
        s_ref[...] = jnp.zeros_like(s_ref)
        r_ref[...] = jnp.zeros_like(r_ref)

    x = x_ref[...]
    mod = mod_ref[...]
    shift, scale = mod[:, :D_MODEL], mod[:, D_MODEL:2 * D_MODEL]
    ms = jnp.mean(x * x, axis=-1, keepdims=True)
    h = x * lax.rsqrt(ms + EPS) * ng_ref[...] * (1.0 + scale) + shift
    hb = h.astype(BF16)
    for n in range(D_IN // PROJ_BLOCK):
        cs = slice(n * PROJ_BLOCK, (n + 1) * PROJ_BLOCK)
        p_ref[:, cs] = _dot(hb, win_ref[:, cs])

    for hd in range(N_HEADS):
        ls = slice(hd * HG_DK, (hd + 1) * HG_DK)
        lbh = lb_ref[:, ls]
        sig = _sigmoid(p_ref[:, OFF_HF + hd * HG_DK:OFF_HF + (hd + 1) * HG_DK])
        f = lbh + (1.0 - lbh) * sig
        k_ref[:, ls] = 1.0 - f
        logf = jnp.log(f)
        hi = logf.astype(BF16)
        lhi_ref[:, ls] = hi
        llo_ref[:, ls] = (logf - hi.astype(F32)).astype(BF16)
    ltri = ltri_ref[...]
    b_ref[...] = _dot(ltri, lhi_ref[...]) + _dot(ltri, llo_ref[...])

    row64 = lax.broadcasted_iota(jnp.int32, (HG_CHUNK, HG_CHUNK), 0)
    col64 = lax.broadcasted_iota(jnp.int32, (HG_CHUNK, HG_CHUNK), 1)
    tril64 = row64 >= col64

    for hd in range(N_HEADS):
        ls = slice(hd * HG_DK, (hd + 1) * HG_DK)
        for c in range(n_hc):
            rs = slice(c * HG_CHUNK, (c + 1) * HG_CHUNK)
            q = _silu(p_ref[rs, OFF_HQ + hd * HG_DK:OFF_HQ + (hd + 1) * HG_DK])
            bb = b_ref[rs, ls]
            qt = (q * jnp.exp(bb)).astype(BF16)
            kt = (k_ref[rs, ls] * jnp.exp(jnp.minimum(-bb, EXP_CLAMP))).astype(BF16)
            a = lax.dot_general(qt, kt, NT_DIMS, preferred_element_type=F32)
            a_ref[hd * n_hc + c] = jnp.where(tril64, a, 0.0)

    worst = jnp.float32(0.0)
    for c in range(n_hc):
        last = b_ref[(c + 1) * HG_CHUNK - 1:(c + 1) * HG_CHUNK, :]
        worst = jnp.maximum(worst, jnp.max(-last))

    @pl.when(worst > EXP_CLAMP)
    def _():
        for hd in range(N_HEADS):
            ls = slice(hd * HG_DK, (hd + 1) * HG_DK)

            def chunk_body(c, carry):
                r0 = pl.multiple_of(c * HG_CHUNK, HG_CHUNK)
                q = _silu(p_ref[pl.ds(r0, HG_CHUNK), OFF_HQ + hd * HG_DK:OFF_HQ + (hd + 1) * HG_DK])
                bb = b_ref[pl.ds(r0, HG_CHUNK), ls]

                def group_body(g, acc):
                    base = pl.multiple_of(r0 + g * SUBLANES, SUBLANES)
                    kg = k_ref[pl.ds(base, SUBLANES), ls]
                    bg = b_ref[pl.ds(base, SUBLANES), ls]
                    for j in range(SUBLANES):
                        e = jnp.exp(jnp.minimum(bb - bg[j:j + 1, :], 0.0))
                        col = jnp.sum(q * kg[j:j + 1, :] * e, axis=-1, keepdims=True)
                        acc = jnp.where(col64 == g * SUBLANES + j, col, acc)
                    return acc

                acc = lax.fori_loop(0, HG_CHUNK // SUBLANES, group_body,
                                    jnp.zeros((HG_CHUNK, HG_CHUNK), F32))
                a_ref[hd * n_hc + c] = jnp.where(tril64, acc, 0.0)
                return carry

            lax.fori_loop(0, n_hc, chunk_body, 0)

    for hd in range(N_HEADS):
        ls = slice(hd * HG_DK, (hd + 1) * HG_DK)
        for c in range(n_hc):
            rs = slice(c * HG_CHUNK, (c + 1) * HG_CHUNK)
            q = _silu(p_ref[rs, OFF_HQ + hd * HG_DK:OFF_HQ + (hd + 1) * HG_DK])
            bb = b_ref[rs, ls]
            blast = bb[HG_CHUNK - 1:HG_CHUNK, :]
            v = p_ref[rs, OFF_HI + hd * HEAD_DV:OFF_HI + (hd + 1) * HEAD_DV].astype(BF16)
            qt = (q * jnp.exp(bb)).astype(BF16)
            st = s_ref[hd]
            intra = _dot(a_ref[hd * n_hc + c].astype(BF16), v)
            inter = lax.dot_general(qt, st.astype(BF16), NT_DIMS, preferred_element_type=F32)
            oa_ref[rs, ls] = intra + inter
            kh = (k_ref[rs, ls] * jnp.exp(blast - bb)).astype(BF16)
            upd = lax.dot_general(v, kh, TN_DIMS, preferred_element_type=F32)
            s_ref[hd] = st * jnp.exp(blast) + upd

    lane128 = lax.broadcasted_iota(jnp.int32, (RET_CHUNK, 2 * RET_DK), 1)
    first_head = lane128 < RET_DK
    bd_row = lax.broadcasted_iota(jnp.int32, (2 * RET_DK, 2 * HEAD_DV), 0)
    bd_col = lax.broadcasted_iota(jnp.int32, (2 * RET_DK, 2 * HEAD_DV), 1)
    bd_mask = (bd_row < RET_DK) == (bd_col < HEAD_DV)
    for pr in range(N_PAIRS):
        qk = slice(pr * 2 * RET_DK, (pr + 1) * 2 * RET_DK)
        vs = slice(pr * 2 * HEAD_DV, (pr + 1) * 2 * HEAD_DV)
        for c in range(n_rc):
            rs = slice(c * RET_CHUNK, (c + 1) * RET_CHUNK)
            cos, sina, sinb = cos_ref[rs, :], sina_ref[rs, :], sinb_ref[rs, :]

            def rot(u):
                return u * cos + pltpu.roll(u, 2 * RET_DK - RET_DK // 2, 1) * sina \
                    + pltpu.roll(u, RET_DK // 2, 1) * sinb

            qr = rot(p_ref[rs, OFF_RQ + qk.start:OFF_RQ + qk.stop])
            kr = rot(p_ref[rs, OFF_RK + qk.start:OFF_RK + qk.stop]) * (RET_DK ** -0.5)
            krb = kr.astype(BF16)
            vpair = p_ref[rs, OFF_RV + vs.start:OFF_RV + vs.stop].astype(BF16)
            q0 = jnp.where(first_head, qr, 0.0).astype(BF16)
            q1 = jnp.where(first_head, 0.0, qr).astype(BF16)
            sc0 = lax.dot_general(q0, krb, NT_DIMS, preferred_element_type=F32) * dm_ref[2 * pr]
            sc1 = lax.dot_general(q1, krb, NT_DIMS, preferred_element_type=F32) * dm_ref[2 * pr + 1]
            intra0 = _dot(sc0.astype(BF16), vpair[:, :HEAD_DV])
            intra1 = _dot(sc1.astype(BF16), vpair[:, HEAD_DV:])
            rbd = r_ref[pr]
            cross = _dot((qr * xi_ref[pr]).astype(BF16), rbd.astype(BF16))
            ob_ref[rs, vs] = jnp.concatenate([intra0, intra1], axis=1) + cross
            kz = (kr * zeta_ref[pr]).astype(BF16)
            cstate = lax.dot_general(kz, vpair, TN_DIMS, preferred_element_type=F32)
            r_ref[pr] = rbd * dec_ref[pr] + jnp.where(bd_mask, cstate, 0.0)

    for hd in range(N_HEADS):
        ls = slice(hd * HEAD_DV, (hd + 1) * HEAD_DV)

        def branch(o, g, z):
            y = o * lax.rsqrt(jnp.mean(o * o, axis=-1, keepdims=True) + EPS) * g
            return y * _silu(z)

        ua = branch(oa_ref[:, ls], hgg_ref[:, ls], p_ref[:, OFF_HZ + ls.start:OFF_HZ + ls.stop])
        ub = branch(ob_ref[:, ls], rtg_ref[:, ls], p_ref[:, OFF_RZ + ls.start:OFF_RZ + ls.stop])
        ga = _sigmoid(p_ref[:, OFF_GA + ls.start:OFF_GA + ls.stop])
        gb = _sigmoid(p_ref[:, OFF_GB + ls.start:OFF_GB + ls.stop])
        m_ref[:, ls] = (ga * ua + gb * ub).astype(BF16)

    gate = mod[:, 2 * D_MODEL:]
    xo = x + gate * _dot(m_ref[...], wout_ref[...])
    o_ref[...] = xo * lax.rsqrt(jnp.mean(xo * xo, axis=-1, keepdims=True) + EPS) * fg_ref[...]


def _rotary_tables(seq_len):
    half = RET_DK // 2
    inv_freq = 1.0 / (ROPE_BASE ** jnp.linspace(0.0, 1.0, half, dtype=F32))
    ang = jnp.arange(seq_len, dtype=jnp.int32).astype(F32)[:, None] * inv_freq[None, :]
    cos, sin = jnp.cos(ang), jnp.sin(ang)
    zero = jnp.zeros_like(sin)
    cos_t = jnp.tile(cos, (1, 4))
    sina_t = jnp.tile(jnp.concatenate([-sin, zero], axis=1), (1, 2))
    sinb_t = jnp.tile(jnp.concatenate([zero, sin], axis=1), (1, 2))
    return cos_t, sina_t, sinb_t


def _retention_tables():
    c = RET_CHUNK
    log_gamma = jnp.log(1.0 - jnp.exp2(-5.0 - jnp.arange(N_HEADS, dtype=F32)))
    idx = jnp.arange(c, dtype=F32)
    rel = idx[:, None] - idx[None, :]
    dm = jnp.where(rel >= 0, jnp.exp(log_gamma[:, None, None] * jnp.maximum(rel, 0.0)), 0.0)
    zeta = jnp.exp(log_gamma[:, None] * (c - 1.0 - idx))
    xi = jnp.exp(log_gamma[:, None] * (idx + 1.0))
    decay = jnp.exp(log_gamma * c)

    def per_pair_lanes(t):
        t = t.reshape(N_PAIRS, 2, c)
        return jnp.repeat(jnp.transpose(t, (0, 2, 1)), RET_DK, axis=2)

    dec = jnp.repeat(decay.reshape(N_PAIRS, 1, 2), HEAD_DV, axis=2)
    return dm, per_pair_lanes(xi), per_pair_lanes(zeta), dec


def _chunk_tril(tile_t):
    r = jnp.arange(tile_t, dtype=jnp.int32)
    same_chunk = (r[:, None] // HG_CHUNK) == (r[None, :] // HG_CHUNK)
    return jnp.where(same_chunk & (r[:, None] >= r[None, :]), 1.0, 0.0).astype(BF16)


def _const_spec(shape, single_buffer=False):
    zeros = (0,) * len(shape)
    if single_buffer:
        return pl.BlockSpec(shape, lambda b, t: zeros, pipeline_mode=pl.Buffered(1))
    return pl.BlockSpec(shape, lambda b, t: zeros)


def _prep_call(c, w_ada, b_ada, lb_logits):
    bsz = c.shape[0]
    return pl.pallas_call(
        _prep_kernel,
        out_shape=(jax.ShapeDtypeStruct((bsz, 3 * D_MODEL), F32),
                   jax.ShapeDtypeStruct((1, N_HEADS * HG_DK), F32)),
        compiler_params=pltpu.CompilerParams(vmem_limit_bytes=V7X_VMEM_LIMIT_BYTES),
        name="adaln_prep",
    )(c, w_ada, b_ada.reshape(1, -1), lb_logits)


def _layer_call(x, mod, norm_g, w_in, lb, hg_g, ret_g, final_g, w_out):
    bsz, seq, d = x.shape
    tile_t = min(TILE_T, seq)
    assert d == D_MODEL and seq % tile_t == 0 and tile_t % RET_CHUNK == 0
    n_hc = tile_t // HG_CHUNK
    cos_t, sina_t, sinb_t = _rotary_tables(seq)
    dm, xi, zeta, dec = _retention_tables()
    ltri = _chunk_tril(tile_t)

    row = lambda v: v.reshape(1, -1)
    tile_spec = pl.BlockSpec((None, tile_t, d), lambda b, t: (b, t, 0))
    rot_spec = pl.BlockSpec((tile_t, 2 * RET_DK), lambda b, t: (t, 0))
    in_specs = [
        tile_spec,
        pl.BlockSpec((None, 1, 3 * d), lambda b, t: (b, 0, 0)),
        _const_spec((1, d)),
        _const_spec((d, D_IN), single_buffer=True),
        _const_spec((1, d)), _const_spec((1, d)), _const_spec((1, d)), _const_spec((1, d)),
        _const_spec((d, d), single_buffer=True),
        rot_spec, rot_spec, rot_spec,
        _const_spec(dm.shape), _const_spec(xi.shape), _const_spec(zeta.shape), _const_spec(dec.shape),
        _const_spec(ltri.shape),
    ]
    scratch = [
        pltpu.VMEM((tile_t, D_IN), F32),
        pltpu.VMEM((tile_t, d), F32),
        pltpu.VMEM((tile_t, d), F32),
        pltpu.VMEM((tile_t, d), BF16),
        pltpu.VMEM((tile_t, d), BF16),
        pltpu.VMEM((N_HEADS * n_hc, HG_CHUNK, HG_CHUNK), F32),
        pltpu.VMEM((N_HEADS, HEAD_DV, HG_DK), F32),
        pltpu.VMEM((N_PAIRS, 2 * RET_DK, 2 * HEAD_DV), F32),
        pltpu.VMEM((tile_t, d), F32),
        pltpu.VMEM((tile_t, d), F32),
        pltpu.VMEM((tile_t, d), BF16),
    ]
    return pl.pallas_call(
        _layer_kernel,
        grid=(bsz, seq // tile_t),
        in_specs=in_specs,
        out_specs=tile_spec,
        out_shape=jax.ShapeDtypeStruct(x.shape, x.dtype),
        scratch_shapes=scratch,
        compiler_params=pltpu.CompilerParams(
            dimension_semantics=("arbitrary", "arbitrary"),
            vmem_limit_bytes=V7X_VMEM_LIMIT_BYTES),
        name="hybrid_layer",
    )(x, mod.reshape(bsz, 1, 3 * d), row(norm_g), w_in, lb, row(hg_g), row(ret_g), row(final_g), w_out,
      cos_t, sina_t, sinb_t, dm, xi, zeta, dec, ltri)


def kernel(x, c, norm_g, w_ada, b_ada, w_in, hg_lb_logits, hg_norm_g, ret_norm_g, w_out, final_g):
    depth = norm_g.shape[0]
    assert depth == 1 and hg_lb_logits.shape[0] == 2
    mod, lb = _prep_call(c, w_ada[0], b_ada[0], hg_lb_logits)
    return _layer_call(x, mod, norm_g[0], w_in[0].astype(BF16), lb, hg_norm_g[0], ret_norm_g[0],
                       final_g, w_out[0].astype(BF16))
```

```python
import functools

import jax
import jax.numpy as jnp
from jax import lax
from jax.experimental import pallas as pl
from jax.experimental.pallas import tpu as pltpu

F32 = jnp.float32
BF16 = jnp.bfloat16

D_MODEL = 1024
N_HEADS = 8
HEAD_DV = 128
HG_DK = 128
RET_DK = 64
ROPE_BASE = 10000.0
EPS = 1e-6

TILE_T = 256
HG_CHUNK = 64
RET_CHUNK = 128
N_PAIRS = N_HEADS // 2
SUBLANES = 8

OFF_HQ, OFF_HF, OFF_HI, OFF_HZ = 0, 1024, 2048, 3072
OFF_RQ, OFF_RK, OFF_RV, OFF_RZ = 4096, 4608, 5120, 6144
OFF_GA, OFF_GB = 7168, 8192
D_IN = 9216
PROJ_BLOCK = 1024

EXP_CLAMP = 80.0

V7X_VMEM_LIMIT_BYTES = 58 * 1024 * 1024

NT_DIMS = (((1,), (1,)), ((), ()))
TN_DIMS = (((0,), (0,)), ((), ()))


def _sigmoid(x):
    return 0.5 * (jnp.tanh(0.5 * x) + 1.0)


def _silu(x):
    return x * _sigmoid(x)


def _dot(a, b):
    return jnp.dot(a, b, preferred_element_type=F32)


def _prep_kernel(c_ref, wada_ref, bada_ref, lbl_ref, mod_ref, lb_ref):
    c = c_ref[...]
    sc = _silu(c).astype(BF16)
    mod_ref[...] = _dot(sc, wada_ref[...].astype(BF16)) + bada_ref[...]
    lg = lbl_ref[...]
    l0, l1 = lg[0:1, :], lg[1:2, :]
    mx = jnp.maximum(l0, l1)
    e0, e1 = jnp.exp(l0 - mx), jnp.exp(l1 - mx)
    lb_ref[...] = e0 / (e0 + e1)


def _layer_kernel(x_ref, mod_ref, ng_ref, win_ref, lb_ref, hgg_ref, rtg_ref, fg_ref, wout_ref,
                  cos_ref, sina_ref, sinb_ref, dm_ref, xi_ref, zeta_ref, dec_ref, ltri_ref,
                  o_ref,
                  p_ref, k_ref, b_ref, lhi_ref, llo_ref, a_ref, s_ref, r_ref, oa_ref, ob_ref, m_ref):
    T = x_ref.shape[0]
    n_hc = T // HG_CHUNK
    n_rc = T // RET_CHUNK

    @pl.when(pl.program_id(1) == 0)
    def _():
        s_ref[...] = jnp.zeros_like(s_ref)
        r_ref[...] = jnp.zeros_like(r_ref)

    x = x_ref[...]
    mod = mod_ref[...]
    shift, scale = mod[:, :D_MODEL], mod[:, D_MODEL:2 * D_MODEL]
    ms = jnp.mean(x * x, axis=-1, keepdims=True)
    h = x * lax.rsqrt(ms + EPS) * ng_ref[...] * (1.0 + scale) + shift
    hb = h.astype(BF16)
    for n in range(D_IN // PROJ_BLOCK):
        cs = slice(n * PROJ_BLOCK, (n + 1) * PROJ_BLOCK)
        p_ref[:, cs] = _dot(hb, win_ref[:, cs])

    for hd in range(N_HEADS):
        ls = slice(hd * HG_DK, (hd + 1) * HG_DK)
        lbh = lb_ref[:, ls]
        sig = _sigmoid(p_ref[:, OFF_HF + hd * HG_DK:OFF_HF + (hd + 1) * HG_DK])
        f = lbh + (1.0 - lbh) * sig
        k_ref[:, ls] = 1.0 - f
        logf = jnp.log(f)
        hi = logf.astype(BF16)
        lhi_ref[:, ls] = hi
        llo_ref[:, ls] = (logf - hi.astype(F32)).astype(BF16)
    ltri = ltri_ref[...]
    b_ref[...] = _dot(ltri, lhi_ref[...]) + _dot(ltri, llo_ref[...])

    row64 = lax.broadcasted_iota(jnp.int32, (HG_CHUNK, HG_CHUNK), 0)
    col64 = lax.broadcasted_iota(jnp.int32, (HG_CHUNK, HG_CHUNK), 1)
    tril64 = row64 >= col64

    for hd in range(N_HEADS):
        ls = slice(hd * HG_DK, (hd + 1) * HG_DK)
        for c in range(n_hc):
            rs = slice(c * HG_CHUNK, (c + 1) * HG_CHUNK)
            q = _silu(p_ref[rs, OFF_HQ + hd * HG_DK:OFF_HQ + (hd + 1) * HG_DK])
            bb = b_ref[rs, ls]
            qt = (q * jnp.exp(bb)).astype(BF16)
            kt = (k_ref[rs, ls] * jnp.exp(jnp.minimum(-bb, EXP_CLAMP))).astype(BF16)
            a = lax.dot_general(qt, kt, NT_DIMS, preferred_element_type=F32)
            a_ref[hd * n_hc + c] = jnp.where(tril64, a, 0.0)

    worst = jnp.float32(0.0)
    for c in range(n_hc):
        last = b_ref[(c + 1) * HG_CHUNK - 1:(c + 1) * HG_CHUNK, :]
        worst = jnp.maximum(worst, jnp.max(-last))

    @pl.when(worst > EXP_CLAMP)
    def _():
        for hd in range(N_HEADS):
            ls = slice(hd * HG_DK, (hd + 1) * HG_DK)

            def chunk_body(c, carry):
                r0 = pl.multiple_of(c * HG_CHUNK, HG_CHUNK)
                q = _silu(p_ref[pl.ds(r0, HG_CHUNK), OFF_HQ + hd * HG_DK:OFF_HQ + (hd + 1) * HG_DK])
                bb = b_ref[pl.ds(r0, HG_CHUNK), ls]

                def group_body(g, acc):
                    base = pl.multiple_of(r0 + g * SUBLANES, SUBLANES)
                    kg = k_ref[pl.ds(base, SUBLANES), ls]
                    bg = b_ref[pl.ds(base, SUBLANES), ls]
                    for j in range(SUBLANES):
                        e = jnp.exp(jnp.minimum(bb - bg[j:j + 1, :], 0.0))
                        col = jnp.sum(q * kg[j:j + 1, :] * e, axis=-1, keepdims=True)
                        acc = jnp.where(col64 == g * SUBLANES + j, col, acc)
                    return acc

                acc = lax.fori_loop(0, HG_CHUNK // SUBLANES, group_body,
                                    jnp.zeros((HG_CHUNK, HG_CHUNK), F32))
                a_ref[hd * n_hc + c] = jnp.where(tril64, acc, 0.0)
                return carry

            lax.fori_loop(0, n_hc, chunk_body, 0)

    for hd in range(N_HEADS):
        ls = slice(hd * HG_DK, (hd + 1) * HG_DK)
        for c in range(n_hc):
            rs = slice(c * HG_CHUNK, (c + 1) * HG_CHUNK)
            q = _silu(p_ref[rs, OFF_HQ + hd * HG_DK:OFF_HQ + (hd + 1) * HG_DK])
            bb = b_ref[rs, ls]
            blast = bb[HG_CHUNK - 1:HG_CHUNK, :]
            v = p_ref[rs, OFF_HI + hd * HEAD_DV:OFF_HI + (hd + 1) * HEAD_DV].astype(BF16)
            qt = (q * jnp.exp(bb)).astype(BF16)
            st = s_ref[hd]
            intra = _dot(a_ref[hd * n_hc + c].astype(BF16), v)
            inter = lax.dot_general(qt, st.astype(BF16), NT_DIMS, preferred_element_type=F32)
            oa_ref[rs, ls] = intra + inter
            kh = (k_ref[rs, ls] * jnp.exp(blast - bb)).astype(BF16)
            upd = lax.dot_general(v, kh, TN_DIMS, preferred_element_type=F32)
            s_ref[hd] = st * jnp.exp(blast) + upd

    lane128 = lax.broadcasted_iota(jnp.int32, (RET_CHUNK, 2 * RET_DK), 1)
    first_head = lane128 < RET_DK
    bd_row = lax.broadcasted_iota(jnp.int32, (2 * RET_DK, 2 * HEAD_DV), 0)
    bd_col = lax.broadcasted_iota(jnp.int32, (2 * RET_DK, 2 * HEAD_DV), 1)
    bd_mask = (bd_row < RET_DK) == (bd_col < HEAD_DV)
    for pr in range(N_PAIRS):
        qk = slice(pr * 2 * RET_DK, (pr + 1) * 2 * RET_DK)
        vs = slice(pr * 2 * HEAD_DV, (pr + 1) * 2 * HEAD_DV)
        for c in range(n_rc):
            rs = slice(c * RET_CHUNK, (c + 1) * RET_CHUNK)
            cos, sina, sinb = cos_ref[rs, :], sina_ref[rs, :], sinb_ref[rs, :]

            def rot(u):
                return u * cos + pltpu.roll(u, 2 * RET_DK - RET_DK // 2, 1) * sina \
                    + pltpu.roll(u, RET_DK // 2, 1) * sinb

            qr = rot(p_ref[rs, OFF_RQ + qk.start:OFF_RQ + qk.stop])
            kr = rot(p_ref[rs, OFF_RK + qk.start:OFF_RK + qk.stop]) * (RET_DK ** -0.5)
            krb = kr.astype(BF16)
            vpair = p_ref[rs, OFF_RV + vs.start:OFF_RV + vs.stop].astype(BF16)
            q0 = jnp.where(first_head, qr, 0.0).astype(BF16)
            q1 = jnp.where(first_head, 0.0, qr).astype(BF16)
            sc0 = lax.dot_general(q0, krb, NT_DIMS, preferred_element_type=F32) * dm_ref[2 * pr]
            sc1 = lax.dot_general(q1, krb, NT_DIMS, preferred_element_type=F32) * dm_ref[2 * pr + 1]
            intra0 = _dot(sc0.astype(BF16), vpair[:, :HEAD_DV])
            intra1 = _dot(sc1.astype(BF16), vpair[:, HEAD_DV:])
            rbd = r_ref[pr]
            cross = _dot((qr * xi_ref[pr]).astype(BF16), rbd.astype(BF16))
            ob_ref[rs, vs] = jnp.concatenate([intra0, intra1], axis=1) + cross
            kz = (kr * zeta_ref[pr]).astype(BF16)
            cstate = lax.dot_general(kz, vpair, TN_DIMS, preferred_element_type=F32)
            r_ref[pr] = rbd * dec_ref[pr] + jnp.where(bd_mask, cstate, 0.0)

    for hd in range(N_HEADS):
        ls = slice(hd * HEAD_DV, (hd + 1) * HEAD_DV)

        def branch(o, g, z):
            y = o * lax.rsqrt(jnp.mean(o * o, axis=-1, keepdims=True) + EPS) * g
            return y * _silu(z)

        ua = branch(oa_ref[:, ls], hgg_ref[:, ls], p_ref[:, OFF_HZ + ls.start:OFF_HZ + ls.stop])
        ub = branch(ob_ref[:, ls], rtg_ref[:, ls], p_ref[:, OFF_RZ + ls.start:OFF_RZ + ls.stop])
        ga = _sigmoid(p_ref[:, OFF_GA + ls.start:OFF_GA + ls.stop])
        gb = _sigmoid(p_ref[:, OFF_GB + ls.start:OFF_GB + ls.stop])
        m_ref[:, ls] = (ga * ua + gb * ub).astype(BF16)

    gate = mod[:, 2 * D_MODEL:]
    xo = x + gate * _dot(m_ref[...], wout_ref[...])
    o_ref[...] = xo * lax.rsqrt(jnp.mean(xo * xo, axis=-1, keepdims=True) + EPS) * fg_ref[...]


def _rotary_tables(seq_len):
    half = RET_DK // 2
    inv_freq = 1.0 / (ROPE_BASE ** jnp.linspace(0.0, 1.0, half, dtype=F32))
    ang = jnp.arange(seq_len, dtype=jnp.int32).astype(F32)[:, None] * inv_freq[None, :]
    cos, sin = jnp.cos(ang), jnp.sin(ang)
    zero = jnp.zeros_like(sin)
    cos_t = jnp.tile(cos, (1, 4))
    sina_t = jnp.tile(jnp.concatenate([-sin, zero], axis=1), (1, 2))
    sinb_t = jnp.tile(jnp.concatenate([zero, sin], axis=1), (1, 2))
    return cos_t, sina_t, sinb_t


def _retention_tables():
    c = RET_CHUNK
    log_gamma = jnp.log(1.0 - jnp.exp2(-5.0 - jnp.arange(N_HEADS, dtype=F32)))
    idx = jnp.arange(c, dtype=F32)
    rel = idx[:, None] - idx[None, :]
    dm = jnp.where(rel >= 0, jnp.exp(log_gamma[:, None, None] * jnp.maximum(rel, 0.0)), 0.0)
    zeta = jnp.exp(log_gamma[:, None] * (c - 1.0 - idx))
    xi = jnp.exp(log_gamma[:, None] * (idx + 1.0))
    decay = jnp.exp(log_gamma * c)

    def per_pair_lanes(t):
        t = t.reshape(N_PAIRS, 2, c)
        return jnp.repeat(jnp.transpose(t, (0, 2, 1)), RET_DK, axis=2)

    dec = jnp.repeat(decay.reshape(N_PAIRS, 1, 2), HEAD_DV, axis=2)
    return dm, per_pair_lanes(xi), per_pair_lanes(zeta), dec


def _chunk_tril(tile_t):
    r = jnp.arange(tile_t, dtype=jnp.int32)
    same_chunk = (r[:, None] // HG_CHUNK) == (r[None, :] // HG_CHUNK)
    return jnp.where(same_chunk & (r[:, None] >= r[None, :]), 1.0, 0.0).astype(BF16)


def _const_spec(shape, single_buffer=False):
    zeros = (0,) * len(shape)
    if single_buffer:
        return pl.BlockSpec(shape, lambda b, t: zeros, pipeline_mode=pl.Buffered(1))
    return pl.BlockSpec(shape, lambda b, t: zeros)


def _prep_call(c, w_ada, b_ada, lb_logits):
    bsz = c.shape[0]
    return pl.pallas_call(
        _prep_kernel,
        out_shape=(jax.ShapeDtypeStruct((bsz, 3 * D_MODEL), F32),
                   jax.ShapeDtypeStruct((1, N_HEADS * HG_DK), F32)),
        compiler_params=pltpu.CompilerParams(vmem_limit_bytes=V7X_VMEM_LIMIT_BYTES),
        name="adaln_prep",
    )(c, w_ada, b_ada.reshape(1, -1), lb_logits)


def _layer_call(x, mod, norm_g, w_in, lb, hg_g, ret_g, final_g, w_out):
    bsz, seq, d = x.shape
    tile_t = min(TILE_T, seq)
    assert d == D_MODEL and seq % tile_t == 0 and tile_t % RET_CHUNK == 0
    n_hc = tile_t // HG_CHUNK
    cos_t, sina_t, sinb_t = _rotary_tables(seq)
    dm, xi, zeta, dec = _retention_tables()
    ltri = _chunk_tril(tile_t)

    row = lambda v: v.reshape(1, -1)
    tile_spec = pl.BlockSpec((None, tile_t, d), lambda b, t: (b, t, 0))
    rot_spec = pl.BlockSpec((tile_t, 2 * RET_DK), lambda b, t: (t, 0))
    in_specs = [
        tile_spec,
        pl.BlockSpec((None, 1, 3 * d), lambda b, t: (b, 0, 0)),
        _const_spec((1, d)),
        _const_spec((d, D_IN), single_buffer=True),
        _const_spec((1, d)), _const_spec((1, d)), _const_spec((1, d)), _const_spec((1, d)),
        _const_spec((d, d), single_buffer=True),
        rot_spec, rot_spec, rot_spec,
        _const_spec(dm.shape), _const_spec(xi.shape), _const_spec(zeta.shape), _const_spec(dec.shape),
        _const_spec(ltri.shape),
    ]
    scratch = [
        pltpu.VMEM((tile_t, D_IN), F32),
        pltpu.VMEM((tile_t, d), F32),
        pltpu.VMEM((tile_t, d), F32),
        pltpu.VMEM((tile_t, d), BF16),
        pltpu.VMEM((tile_t, d), BF16),
        pltpu.VMEM((N_HEADS * n_hc, HG_CHUNK, HG_CHUNK), F32),
        pltpu.VMEM((N_HEADS, HEAD_DV, HG_DK), F32),
        pltpu.VMEM((N_PAIRS, 2 * RET_DK, 2 * HEAD_DV), F32),
        pltpu.VMEM((tile_t, d), F32),
        pltpu.VMEM((tile_t, d), F32),
        pltpu.VMEM((tile_t, d), BF16),
    ]
    return pl.pallas_call(
        _layer_kernel,
        grid=(bsz, seq // tile_t),
        in_specs=in_specs,
        out_specs=tile_spec,
        out_shape=jax.ShapeDtypeStruct(x.shape, x.dtype),
        scratch_shapes=scratch,
        compiler_params=pltpu.CompilerParams(
            dimension_semantics=("arbitrary", "arbitrary"),
            vmem_limit_bytes=V7X_VMEM_LIMIT_BYTES),
        name="hybrid_layer",
    )(x, mod.reshape(bsz, 1, 3 * d), row(norm_g), w_in, lb, row(hg_g), row(ret_g), row(final_g), w_out,
      cos_t, sina_t, sinb_t, dm, xi, zeta, dec, ltri)


def kernel(x, c, norm_g, w_ada, b_ada, w_in, hg_lb_logits, hg_norm_g, ret_norm_g, w_out, final_g):
    depth = norm_g.shape[0]
    assert depth == 1 and hg_lb_logits.shape[0] == 2
    mod, lb = _prep_call(c, w_ada[0], b_ada[0], hg_lb_logits)
    return _layer_call(x, mod, norm_g[0], w_in[0].astype(BF16), lb, hg_norm_g[0], ret_norm_g[0],
                       final_g, w_out[0].astype(BF16))
```

```python
import jax
import jax.numpy as jnp
from jax import lax
from jax.experimental import pallas as pl
from jax.experimental.pallas import tpu as pltpu

F32 = jnp.float32
BF16 = jnp.bfloat16

D_MODEL = 1024
N_HEADS = 8
HEAD_DV = 128
HG_DK = 128
RET_DK = 64
ROPE_BASE = 10000.0
EPS = 1e-6

TILE_T = 256
HG_CHUNK = 64
RET_CHUNK = 128
N_PAIRS = N_HEADS // 2
SUBLANES = 8
LANES = 128

OFF_HQ, OFF_HF, OFF_HI, OFF_HZ = 0, 1024, 2048, 3072
OFF_RQ, OFF_RK, OFF_RV, OFF_RZ = 4096, 4608, 5120, 6144
OFF_GA, OFF_GB = 7168, 8192
D_IN = 9216
PROJ_BLOCK = 256

HG_QT, HG_KH, HG_V, HG_WIDTH = 0, 1024, 2048, 3072
RT_Q, RT_K, RT_KZ, RT_QXI, RT_V, RT_WIDTH = 0, 512, 1024, 1536, 2048, 3072
GT_A, GT_B, GT_WIDTH = 0, 1024, 2048

EXP_CLAMP = 80.0

V7X_VMEM_LIMIT_BYTES = 58 * 1024 * 1024

NT_DIMS = (((1,), (1,)), ((), ()))
TN_DIMS = (((0,), (0,)), ((), ()))


def _sigmoid(x):
    return 0.5 * (jnp.tanh(0.5 * x) + 1.0)


def _silu(x):
    return x * _sigmoid(x)


def _dot(a, b):
    return jnp.dot(a, b, preferred_element_type=F32)


def _prep_kernel(c_ref, wada_ref, bada_ref, lbl_ref, mod_ref, lb_ref):
    c = c_ref[...]
    sc = _silu(c).astype(BF16)
    mod_ref[...] = _dot(sc, wada_ref[...].astype(BF16)) + bada_ref[...]
    lg = lbl_ref[...]
    l0, l1 = lg[0:1, :], lg[1:2, :]
    mx = jnp.maximum(l0, l1)
    e0, e1 = jnp.exp(l0 - mx), jnp.exp(l1 - mx)
    lb_ref[...] = e0 / (e0 + e1)


class _ProjectionStream:
    def __init__(self, hb_ref, win_ref, p_ref, pieces):
        self.hb_ref, self.win_ref, self.p_ref = hb_ref, win_ref, p_ref
        self.pieces = list(pieces)

    def emit(self, n_pieces=1):
        for _ in range(min(n_pieces, len(self.pieces))):
            cs = slice(self.pieces[0] * PROJ_BLOCK, (self.pieces[0] + 1) * PROJ_BLOCK)
            self.pieces.pop(0)
            self.p_ref[:, cs] = _dot(self.hb_ref[...], self.win_ref[:, cs])


def _pieces(offset, width):
    return list(range(offset // PROJ_BLOCK, (offset + width) // PROJ_BLOCK))


def _every(n_calls, fn):
    count = [0]

    def tick():
        count[0] += 1
        if count[0] % n_calls == 0:
            fn()

    return tick


def _normalize(x, mod, ng_ref, hb_ref):
    shift, scale = mod[:, :D_MODEL], mod[:, D_MODEL:2 * D_MODEL]
    ms = jnp.mean(x * x, axis=-1, keepdims=True)
    h = x * lax.rsqrt(ms + EPS) * ng_ref[...] * (1.0 + scale) + shift
    hb_ref[...] = h.astype(BF16)


def _forget_gates(p_ref, lb_ref, k_ref, lhi_ref, llo_ref, tick):
    for hd in range(N_HEADS):
        ls = slice(hd * HG_DK, (hd + 1) * HG_DK)
        lbh = lb_ref[:, ls]
        f = lbh + (1.0 - lbh) * _sigmoid(p_ref[:, OFF_HF + ls.start:OFF_HF + ls.stop])
        k_ref[:, ls] = 1.0 - f
        logf = jnp.log(f)
        hi = logf.astype(BF16)
        lhi_ref[:, ls] = hi
        llo_ref[:, ls] = (logf - hi.astype(F32)).astype(BF16)
        tick()


def _gate_product(p_ref, g_off, z_off, gate_ref, out_off, tick):
    for hd in range(N_HEADS):
        ls = slice(hd * HEAD_DV, (hd + 1) * HEAD_DV)
        g = _sigmoid(p_ref[:, g_off + ls.start:g_off + ls.stop])
        z = _silu(p_ref[:, z_off + ls.start:z_off + ls.stop])
        gate_ref[:, out_off + ls.start:out_off + ls.stop] = (g * z).astype(BF16)
        tick()


def _padded_tril():
    row = lax.broadcasted_iota(jnp.int32, (HG_CHUNK, LANES), 0)
    col = lax.broadcasted_iota(jnp.int32, (HG_CHUNK, LANES), 1)
    return row >= col, col


def _hgrn_operands(p_ref, k_ref, b_ref, hg_ref, amat_ref, dec_ref, tick):
    T = p_ref.shape[0]
    tril, _ = _padded_tril()
    zero_rows = jnp.zeros((LANES - HG_CHUNK, HG_DK), BF16)
    for c in range(T // HG_CHUNK):
        rs = slice(c * HG_CHUNK, (c + 1) * HG_CHUNK)
        for hd in range(N_HEADS):
            ls = slice(hd * HG_DK, (hd + 1) * HG_DK)
            q = _silu(p_ref[rs, OFF_HQ + ls.start:OFF_HQ + ls.stop])
            bb = b_ref[rs, ls]
            blast = bb[HG_CHUNK - 1:HG_CHUNK, :]
            qt = q * jnp.exp(bb)
            kh = (k_ref[rs, ls] * jnp.exp(blast - bb)).astype(BF16)
            hg_ref[rs, HG_QT + ls.start:HG_QT + ls.stop] = qt.astype(BF16)
            hg_ref[rs, HG_KH + ls.start:HG_KH + ls.stop] = kh
            hg_ref[rs, HG_V + ls.start:HG_V + ls.stop] = p_ref[rs, OFF_HI + ls.start:OFF_HI + ls.stop].astype(BF16)
            dec_ref[c, :, ls] = jnp.exp(blast)
            qh = (qt * jnp.exp(jnp.minimum(-blast, EXP_CLAMP))).astype(BF16)
            kh_pad = jnp.concatenate([kh, zero_rows], axis=0)
            a = lax.dot_general(qh, kh_pad, NT_DIMS, preferred_element_type=F32)
            amat_ref[rs, ls] = jnp.where(tril, a, 0.0).astype(BF16)
            tick()


def _hgrn_scores_exact(p_ref, k_ref, b_ref, amat_ref):
    T = p_ref.shape[0]
    tril, col = _padded_tril()
    for hd in range(N_HEADS):
        ls = slice(hd * HG_DK, (hd + 1) * HG_DK)

        def chunk_body(c, carry):
            r0 = pl.multiple_of(c * HG_CHUNK, HG_CHUNK)
            rows = pl.ds(r0, HG_CHUNK)
            q = _silu(p_ref[rows, OFF_HQ + ls.start:OFF_HQ + ls.stop])
            bb = b_ref[rows, ls]

            def group_body(g, acc):
                base = pl.multiple_of(r0 + g * SUBLANES, SUBLANES)
                kg = k_ref[pl.ds(base, SUBLANES), ls]
                bg = b_ref[pl.ds(base, SUBLANES), ls]
                for j in range(SUBLANES):
                    e = jnp.exp(jnp.minimum(bb - bg[j:j + 1, :], 0.0))
                    colv = jnp.sum(q * kg[j:j + 1, :] * e, axis=-1, keepdims=True)
                    acc = jnp.where(col == g * SUBLANES + j, colv, acc)
                return acc

            a = lax.fori_loop(0, HG_CHUNK // SUBLANES, group_body, jnp.zeros((HG_CHUNK, LANES), F32))
            amat_ref[rows, ls] = jnp.where(tril, a, 0.0).astype(BF16)
            return carry

        lax.fori_loop(0, T // HG_CHUNK, chunk_body, 0)


def _retention_operands(p_ref, cos_ref, sina_ref, sinb_ref, xi_ref, zeta_ref, ret_ref):
    cos, sina, sinb = cos_ref[...], sina_ref[...], sinb_ref[...]

    def rot(u):
        return u * cos + pltpu.roll(u, 2 * RET_DK - RET_DK // 2, 1) * sina + pltpu.roll(u, RET_DK // 2, 1) * sinb

    for pr in range(N_PAIRS):
        qk = slice(pr * 2 * RET_DK, (pr + 1) * 2 * RET_DK)
        qr = rot(p_ref[:, OFF_RQ + qk.start:OFF_RQ + qk.stop])
        kr = rot(p_ref[:, OFF_RK + qk.start:OFF_RK + qk.stop]) * (RET_DK ** -0.5)
        ret_ref[:, RT_Q + qk.start:RT_Q + qk.stop] = qr.astype(BF16)
        ret_ref[:, RT_K + qk.start:RT_K + qk.stop] = kr.astype(BF16)
        ret_ref[:, RT_KZ + qk.start:RT_KZ + qk.stop] = (kr * zeta_ref[pr]).astype(BF16)
        ret_ref[:, RT_QXI + qk.start:RT_QXI + qk.stop] = (qr * xi_ref[pr]).astype(BF16)
    ret_ref[:, RT_V:RT_V + N_HEADS * HEAD_DV] = p_ref[:, OFF_RV:OFF_RV + N_HEADS * HEAD_DV].astype(BF16)


def _front_kernel(x_ref, mod_ref, ng_ref, win_ref, lb_ref, ltri_ref, cos_ref, sina_ref, sinb_ref,
                  xi_ref, zeta_ref,
                  hg_ref, ret_ref, gate_ref, amat_ref, dec_ref,
                  p_ref, hb_ref, k_ref, b_ref, lhi_ref, llo_ref):
    _normalize(x_ref[...], mod_ref[...], ng_ref, hb_ref)
    proj = _ProjectionStream(
        hb_ref, win_ref, p_ref,
        _pieces(OFF_HF, 1024) + _pieces(OFF_HZ, 1024) + _pieces(OFF_GA, 1024)
        + _pieces(OFF_RZ, 1024) + _pieces(OFF_GB, 1024) + _pieces(OFF_HQ, 1024) + _pieces(OFF_HI, 1024)
        + _pieces(OFF_RQ, 1024) + _pieces(OFF_RV, 1024))
    proj.emit(4)
    _forget_gates(p_ref, lb_ref, k_ref, lhi_ref, llo_ref, proj.emit)
    ltri = ltri_ref[...]
    b_ref[...] = _dot(ltri, lhi_ref[...]) + _dot(ltri, llo_ref[...])
    _gate_product(p_ref, OFF_GA, OFF_HZ, gate_ref, GT_A, proj.emit)
    _gate_product(p_ref, OFF_GB, OFF_RZ, gate_ref, GT_B, proj.emit)
    _hgrn_operands(p_ref, k_ref, b_ref, hg_ref, amat_ref, dec_ref, _every(4, proj.emit))
    _retention_operands(p_ref, cos_ref, sina_ref, sinb_ref, xi_ref, zeta_ref, ret_ref)

    worst = jnp.float32(0.0)
    for c in range(x_ref.shape[0] // HG_CHUNK):
        worst = jnp.maximum(worst, jnp.max(-b_ref[(c + 1) * HG_CHUNK - 1:(c + 1) * HG_CHUNK, :]))

    @pl.when(worst > EXP_CLAMP)
    def _():
        _hgrn_scores_exact(p_ref, k_ref, b_ref, amat_ref)


def _hgrn_recurrence(hg_ref, amat_ref, dec_ref, s_ref, oa_ref):
    T = hg_ref.shape[0]
    states = [s_ref[hd] for hd in range(N_HEADS)]
    for c in range(T // HG_CHUNK):
        rs = slice(c * HG_CHUNK, (c + 1) * HG_CHUNK)
        for hd in range(N_HEADS):
            ls = slice(hd * HG_DK, (hd + 1) * HG_DK)
            qt = hg_ref[rs, HG_QT + ls.start:HG_QT + ls.stop]
            kh = hg_ref[rs, HG_KH + ls.start:HG_KH + ls.stop]
            v = hg_ref[rs, HG_V + ls.start:HG_V + ls.stop]
            st = states[hd]
            intra = _dot(amat_ref[rs, ls], jnp.concatenate([v, v], axis=0))
            inter = lax.dot_general(qt, st.astype(BF16), NT_DIMS, preferred_element_type=F32)
            oa_ref[rs, ls] = intra + inter
            states[hd] = st * dec_ref[c, :, ls] + lax.dot_general(v, kh, TN_DIMS, preferred_element_type=F32)
    for hd in range(N_HEADS):
        s_ref[hd] = states[hd]


def _retention_recurrence(ret_ref, dm_ref, rdec_ref, r_ref, ob_ref):
    T = ret_ref.shape[0]
    first_head = lax.broadcasted_iota(jnp.int32, (RET_CHUNK, 2 * RET_DK), 1) < RET_DK
    bd_row = lax.broadcasted_iota(jnp.int32, (2 * RET_DK, 2 * HEAD_DV), 0)
    bd_col = lax.broadcasted_iota(jnp.int32, (2 * RET_DK, 2 * HEAD_DV), 1)
    bd_mask = (bd_row < RET_DK) == (bd_col < HEAD_DV)
    states = [r_ref[pr] for pr in range(N_PAIRS)]
    for c in range(T // RET_CHUNK):
        rs = slice(c * RET_CHUNK, (c + 1) * RET_CHUNK)
        for pr in range(N_PAIRS):
            qk = slice(pr * 2 * RET_DK, (pr + 1) * 2 * RET_DK)
            vs = slice(pr * 2 * HEAD_DV, (pr + 1) * 2 * HEAD_DV)
            rbd = states[pr]
            qr = ret_ref[rs, RT_Q + qk.start:RT_Q + qk.stop]
            kr = ret_ref[rs, RT_K + qk.start:RT_K + qk.stop]
            vpair = ret_ref[rs, RT_V + vs.start:RT_V + vs.stop]
            zero = jnp.zeros_like(qr)
            q0 = jnp.where(first_head, qr, zero)
            q1 = jnp.where(first_head, zero, qr)
            sc0 = lax.dot_general(q0, kr, NT_DIMS, preferred_element_type=F32) * dm_ref[2 * pr]
            sc1 = lax.dot_general(q1, kr, NT_DIMS, preferred_element_type=F32) * dm_ref[2 * pr + 1]
            intra0 = _dot(sc0.astype(BF16), vpair[:, :HEAD_DV])
            intra1 = _dot(sc1.astype(BF16), vpair[:, HEAD_DV:])
            cross = _dot(ret_ref[rs, RT_QXI + qk.start:RT_QXI + qk.stop], rbd.astype(BF16))
            ob_ref[rs, vs] = jnp.concatenate([intra0, intra1], axis=1) + cross
            cstate = lax.dot_general(ret_ref[rs, RT_KZ + qk.start:RT_KZ + qk.stop], vpair, TN_DIMS,
                                     preferred_element_type=F32)
            states[pr] = rbd * rdec_ref[pr] + jnp.where(bd_mask, cstate, 0.0)
    for pr in range(N_PAIRS):
        r_ref[pr] = states[pr]


def _merge_out(x, gate, oa_ref, ob_ref, gate_ref, hgg_ref, rtg_ref, fg_ref, wout_ref, m_ref, o_ref):
    def normed(o, g):
        return o * lax.rsqrt(jnp.mean(o * o, axis=-1, keepdims=True) + EPS) * g

    for hd in range(N_HEADS):
        ls = slice(hd * HEAD_DV, (hd + 1) * HEAD_DV)
        ua = normed(oa_ref[:, ls], hgg_ref[:, ls]) * gate_ref[:, GT_A + ls.start:GT_A + ls.stop]
        ub = normed(ob_ref[:, ls], rtg_ref[:, ls]) * gate_ref[:, GT_B + ls.start:GT_B + ls.stop]
        m_ref[:, ls] = (ua + ub).astype(BF16)
    xo = x + gate * _dot(m_ref[...], wout_ref[...])
    o_ref[...] = xo * lax.rsqrt(jnp.mean(xo * xo, axis=-1, keepdims=True) + EPS) * fg_ref[...]


def _back_kernel(x_ref, mod_ref, hg_ref, ret_ref, gate_ref, amat_ref, dec_ref, hgg_ref, rtg_ref, fg_ref,
                 wout_ref, dm_ref, rdec_ref,
                 o_ref,
                 s_ref, r_ref, oa_ref, ob_ref, m_ref):
    @pl.when(pl.program_id(1) == 0)
    def _():
        s_ref[...] = jnp.zeros_like(s_ref)
        r_ref[...] = jnp.zeros_like(r_ref)

    _hgrn_recurrence(hg_ref, amat_ref, dec_ref, s_ref, oa_ref)
    _retention_recurrence(ret_ref, dm_ref, rdec_ref, r_ref, ob_ref)
    _merge_out(x_ref[...], mod_ref[:, 2 * D_MODEL:], oa_ref, ob_ref, gate_ref, hgg_ref, rtg_ref, fg_ref,
               wout_ref, m_ref, o_ref)


def _rotary_tables(seq_len):
    half = RET_DK // 2
    inv_freq = 1.0 / (ROPE_BASE ** jnp.linspace(0.0, 1.0, half, dtype=F32))
    ang = jnp.arange(seq_len, dtype=jnp.int32).astype(F32)[:, None] * inv_freq[None, :]
    cos, sin = jnp.cos(ang), jnp.sin(ang)
    zero = jnp.zeros_like(sin)
    cos_t = jnp.tile(cos, (1, 4))
    sina_t = jnp.tile(jnp.concatenate([-sin, zero], axis=1), (1, 2))
    sinb_t = jnp.tile(jnp.concatenate([zero, sin], axis=1), (1, 2))
    return cos_t, sina_t, sinb_t


def _retention_tables(tile_t):
    c = RET_CHUNK
    log_gamma = jnp.log(1.0 - jnp.exp2(-5.0 - jnp.arange(N_HEADS, dtype=F32)))
    idx = jnp.arange(c, dtype=F32)
    rel = idx[:, None] - idx[None, :]
    dm = jnp.where(rel >= 0, jnp.exp(log_gamma[:, None, None] * jnp.maximum(rel, 0.0)), 0.0)
    zeta = jnp.exp(log_gamma[:, None] * (c - 1.0 - idx))
    xi = jnp.exp(log_gamma[:, None] * (idx + 1.0))
    decay = jnp.exp(log_gamma * c)

    def per_pair_lanes(t):
        t = jnp.transpose(t.reshape(N_PAIRS, 2, c), (0, 2, 1))
        return jnp.tile(jnp.repeat(t, RET_DK, axis=2), (1, tile_t // c, 1))

    dec = jnp.repeat(decay.reshape(N_PAIRS, 1, 2), HEAD_DV, axis=2)
    return dm, per_pair_lanes(xi), per_pair_lanes(zeta), dec


def _chunk_tril(tile_t):
    r = jnp.arange(tile_t, dtype=jnp.int32)
    same_chunk = (r[:, None] // HG_CHUNK) == (r[None, :] // HG_CHUNK)
    return jnp.where(same_chunk & (r[:, None] >= r[None, :]), 1.0, 0.0).astype(BF16)


def _const_spec(shape, single_buffer=False):
    zeros = (0,) * len(shape)
    if single_buffer:
        return pl.BlockSpec(shape, lambda b, t: zeros, pipeline_mode=pl.Buffered(1))
    return pl.BlockSpec(shape, lambda b, t: zeros)


def _prep_call(c, w_ada, b_ada, lb_logits):
    bsz = c.shape[0]
    return pl.pallas_call(
        _prep_kernel,
        out_shape=(jax.ShapeDtypeStruct((bsz, 3 * D_MODEL), F32),
                   jax.ShapeDtypeStruct((1, N_HEADS * HG_DK), F32)),
        compiler_params=pltpu.CompilerParams(vmem_limit_bytes=V7X_VMEM_LIMIT_BYTES),
        name="adaln_prep",
    )(c, w_ada, b_ada.reshape(1, -1), lb_logits)


def _tile_spec(tile_t, width):
    return pl.BlockSpec((None, tile_t, width), lambda b, t: (b, t, 0))


def _front_call(x, mod3, norm_g, w_in, lb, tile_t):
    bsz, seq, d = x.shape
    n_hc = tile_t // HG_CHUNK
    cos_t, sina_t, sinb_t = _rotary_tables(seq)
    _, xi, zeta, _ = _retention_tables(tile_t)
    ltri = _chunk_tril(tile_t)
    rot_spec = pl.BlockSpec((tile_t, 2 * RET_DK), lambda b, t: (t, 0))
    dec_spec = pl.BlockSpec((None, n_hc, 1, d), lambda b, t: (b, t, 0, 0))
    bf = lambda width: jax.ShapeDtypeStruct((bsz, seq, width), BF16)
    return pl.pallas_call(
        _front_kernel,
        grid=(bsz, seq // tile_t),
        in_specs=[
            _tile_spec(tile_t, d),
            pl.BlockSpec((None, 1, 3 * d), lambda b, t: (b, 0, 0)),
            _const_spec((1, d)),
            _const_spec((d, D_IN), single_buffer=True),
            _const_spec((1, d)),
            _const_spec(ltri.shape),
            rot_spec, rot_spec, rot_spec,
            _const_spec(xi.shape), _const_spec(zeta.shape),
        ],
        out_specs=[
            _tile_spec(tile_t, HG_WIDTH), _tile_spec(tile_t, RT_WIDTH), _tile_spec(tile_t, GT_WIDTH),
            _tile_spec(tile_t, d), dec_spec,
        ],
        out_shape=[bf(HG_WIDTH), bf(RT_WIDTH), bf(GT_WIDTH), bf(d),
                   jax.ShapeDtypeStruct((bsz, seq // HG_CHUNK, 1, d), F32)],
        scratch_shapes=[
            pltpu.VMEM((tile_t, D_IN), F32),
            pltpu.VMEM((tile_t, d), BF16),
            pltpu.VMEM((tile_t, d), F32),
            pltpu.VMEM((tile_t, d), F32),
            pltpu.VMEM((tile_t, d), BF16),
            pltpu.VMEM((tile_t, d), BF16),
        ],
        compiler_params=pltpu.CompilerParams(
            dimension_semantics=("arbitrary", "arbitrary"),
            vmem_limit_bytes=V7X_VMEM_LIMIT_BYTES),
        name="layer_front",
    )(x, mod3, norm_g.reshape(1, -1), w_in, lb, ltri, cos_t, sina_t, sinb_t, xi, zeta)


def _back_call(x, mod3, hg, ret, gate, amat, dec, hg_g, ret_g, final_g, w_out, tile_t):
    bsz, seq, d = x.shape
    n_hc = tile_t // HG_CHUNK
    dm, _, _, rdec = _retention_tables(tile_t)
    row = lambda v: v.reshape(1, -1)
    return pl.pallas_call(
        _back_kernel,
        grid=(bsz, seq // tile_t),
        in_specs=[
            _tile_spec(tile_t, d),
            pl.BlockSpec((None, 1, 3 * d), lambda b, t: (b, 0, 0)),
            _tile_spec(tile_t, HG_WIDTH), _tile_spec(tile_t, RT_WIDTH), _tile_spec(tile_t, GT_WIDTH),
            _tile_spec(tile_t, d),
            pl.BlockSpec((None, n_hc, 1, d), lambda b, t: (b, t, 0, 0)),
            _const_spec((1, d)), _const_spec((1, d)), _const_spec((1, d)),
            _const_spec((d, d), single_buffer=True),
            _const_spec(dm.shape), _const_spec(rdec.shape),
        ],
        out_specs=_tile_spec(tile_t, d),
        out_shape=jax.ShapeDtypeStruct(x.shape, x.dtype),
        scratch_shapes=[
            pltpu.VMEM((N_HEADS, HEAD_DV, HG_DK), F32),
            pltpu.VMEM((N_PAIRS, 2 * RET_DK, 2 * HEAD_DV), F32),
            pltpu.VMEM((tile_t, d), F32),
            pltpu.VMEM((tile_t, d), F32),
            pltpu.VMEM((tile_t, d), BF16),
        ],
        compiler_params=pltpu.CompilerParams(
            dimension_semantics=("arbitrary", "arbitrary"),
            vmem_limit_bytes=V7X_VMEM_LIMIT_BYTES),
        name="layer_back",
    )(x, mod3, hg, ret, gate, amat, dec, row(hg_g), row(ret_g), row(final_g), w_out, dm, rdec)


def kernel(x, c, norm_g, w_ada, b_ada, w_in, hg_lb_logits, hg_norm_g, ret_norm_g, w_out, final_g):
    depth = norm_g.shape[0]
    assert depth == 1 and hg_lb_logits.shape[0] == 2
    bsz, seq, d = x.shape
    tile_t = min(TILE_T, seq)
    assert d == D_MODEL and seq % tile_t == 0 and tile_t % RET_CHUNK == 0
    mod, lb = _prep_call(c, w_ada[0], b_ada[0], hg_lb_logits)
    mod3 = mod.reshape(bsz, 1, 3 * d)
    hg, ret, gate, amat, dec = _front_call(x, mod3, norm_g[0], w_in[0].astype(BF16), lb, tile_t)
    return _back_call(x, mod3, hg, ret, gate, amat, dec, hg_norm_g[0], ret_norm_g[0], final_g,
                      w_out[0].astype(BF16), tile_t)
```

```python
import jax
import jax.numpy as jnp
from jax import lax
from jax.experimental import pallas as pl
from jax.experimental.pallas import tpu as pltpu

F32 = jnp.float32
BF16 = jnp.bfloat16

D_MODEL = 1024
N_HEADS = 8
HEAD_DV = 128
HG_DK = 128
RET_DK = 64
ROPE_BASE = 10000.0
EPS = 1e-6

TILE_T = 256
HG_CHUNK = 64
RET_CHUNK = 128
N_PAIRS = N_HEADS // 2
SUBLANES = 8
LANES = 128

OFF_HQ, OFF_HF, OFF_HI, OFF_HZ = 0, 1024, 2048, 3072
OFF_RQ, OFF_RK, OFF_RV, OFF_RZ = 4096, 4608, 5120, 6144
OFF_GA, OFF_GB = 7168, 8192
D_IN = 9216
PROJ_BLOCK = 256

HG_QT, HG_KH, HG_V, HG_WIDTH = 0, 1024, 2048, 3072
RT_Q, RT_K, RT_KZ, RT_QXI, RT_V, RT_WIDTH = 0, 512, 1024, 1536, 2048, 3072
GT_A, GT_B, GT_WIDTH = 0, 1024, 2048

EXP_CLAMP = 80.0

V7X_VMEM_LIMIT_BYTES = 58 * 1024 * 1024

NT_DIMS = (((1,), (1,)), ((), ()))
TN_DIMS = (((0,), (0,)), ((), ()))


def _sigmoid(x):
    return 0.5 * (jnp.tanh(0.5 * x) + 1.0)


def _silu(x):
    return x * _sigmoid(x)


def _dot(a, b):
    return jnp.dot(a, b, preferred_element_type=F32)


def _prep_kernel(c_ref, wada_ref, bada_ref, lbl_ref, mod_ref, lb_ref):
    c = c_ref[...]
    sc = _silu(c).astype(BF16)
    mod_ref[...] = _dot(sc, wada_ref[...].astype(BF16)) + bada_ref[...]
    lg = lbl_ref[...]
    l0, l1 = lg[0:1, :], lg[1:2, :]
    mx = jnp.maximum(l0, l1)
    e0, e1 = jnp.exp(l0 - mx), jnp.exp(l1 - mx)
    lb_ref[...] = e0 / (e0 + e1)


class _ProjectionStream:
    def __init__(self, hb_ref, win_ref, p_ref, pieces):
        self.hb_ref, self.win_ref, self.p_ref = hb_ref, win_ref, p_ref
        self.pieces = list(pieces)

    def emit(self, n_pieces=1):
        for _ in range(min(n_pieces, len(self.pieces))):
            cs = slice(self.pieces[0] * PROJ_BLOCK, (self.pieces[0] + 1) * PROJ_BLOCK)
            self.pieces.pop(0)
            self.p_ref[:, cs] = _dot(self.hb_ref[...], self.win_ref[:, cs])


def _pieces(offset, width):
    return list(range(offset // PROJ_BLOCK, (offset + width) // PROJ_BLOCK))


def _every(n_calls, fn):
    count = [0]

    def tick():
        count[0] += 1
        if count[0] % n_calls == 0:
            fn()

    return tick


def _normalize(x, mod, ng_ref, hb_ref):
    shift, scale = mod[:, :D_MODEL], mod[:, D_MODEL:2 * D_MODEL]
    ms = jnp.mean(x * x, axis=-1, keepdims=True)
    h = x * lax.rsqrt(ms + EPS) * ng_ref[...] * (1.0 + scale) + shift
    hb_ref[...] = h.astype(BF16)


def _forget_gates(p_ref, lb_ref, k_ref, lhi_ref, llo_ref, tick):
    for hd in range(N_HEADS):
        ls = slice(hd * HG_DK, (hd + 1) * HG_DK)
        lbh = lb_ref[:, ls]
        f = lbh + (1.0 - lbh) * _sigmoid(p_ref[:, OFF_HF + ls.start:OFF_HF + ls.stop])
        k_ref[:, ls] = 1.0 - f
        logf = jnp.log(f)
        hi = logf.astype(BF16)
        lhi_ref[:, ls] = hi
        llo_ref[:, ls] = (logf - hi.astype(F32)).astype(BF16)
        tick()


def _gate_product(p_ref, g_off, z_off, gate_ref, out_off, tick):
    for hd in range(N_HEADS):
        ls = slice(hd * HEAD_DV, (hd + 1) * HEAD_DV)
        g = _sigmoid(p_ref[:, g_off + ls.start:g_off + ls.stop])
        z = _silu(p_ref[:, z_off + ls.start:z_off + ls.stop])
        gate_ref[:, out_off + ls.start:out_off + ls.stop] = (g * z).astype(BF16)
        tick()


def _padded_tril():
    row = lax.broadcasted_iota(jnp.int32, (HG_CHUNK, LANES), 0)
    col = lax.broadcasted_iota(jnp.int32, (HG_CHUNK, LANES), 1)
    return row >= col, col


def _hgrn_operands(p_ref, k_ref, b_ref, hg_ref, amat_ref, dcol_ref, tick):
    T = p_ref.shape[0]
    tril = (lax.broadcasted_iota(jnp.int32, (HG_CHUNK, HG_CHUNK), 0)
            >= lax.broadcasted_iota(jnp.int32, (HG_CHUNK, HG_CHUNK), 1))
    amat_ref[...] = jnp.zeros_like(amat_ref)
    for c in range(T // HG_CHUNK):
        rs = slice(c * HG_CHUNK, (c + 1) * HG_CHUNK)
        for hd in range(N_HEADS):
            ls = slice(hd * HG_DK, (hd + 1) * HG_DK)
            q = _silu(p_ref[rs, OFF_HQ + ls.start:OFF_HQ + ls.stop])
            bb = b_ref[rs, ls]
            blast = bb[HG_CHUNK - 1:HG_CHUNK, :]
            qt = q * jnp.exp(bb)
            kh = (k_ref[rs, ls] * jnp.exp(blast - bb)).astype(BF16)
            hg_ref[rs, HG_QT + ls.start:HG_QT + ls.stop] = qt.astype(BF16)
            hg_ref[rs, HG_KH + ls.start:HG_KH + ls.stop] = kh
            hg_ref[rs, HG_V + ls.start:HG_V + ls.stop] = p_ref[rs, OFF_HI + ls.start:OFF_HI + ls.stop].astype(BF16)
            dcol_ref[c, ls, :] = jnp.transpose(jnp.broadcast_to(jnp.exp(blast), (HG_DK, HEAD_DV)))
            qh = (qt * jnp.exp(jnp.minimum(-blast, EXP_CLAMP))).astype(BF16)
            a = lax.dot_general(qh, kh, NT_DIMS, preferred_element_type=F32)
            amat_ref[rs, ls.start:ls.start + HG_CHUNK] = jnp.where(tril, a, 0.0).astype(BF16)
            tick()


def _hgrn_scores_exact(p_ref, k_ref, b_ref, amat_ref):
    T = p_ref.shape[0]
    tril, col = _padded_tril()
    for hd in range(N_HEADS):
        ls = slice(hd * HG_DK, (hd + 1) * HG_DK)

        def chunk_body(c, carry):
            r0 = pl.multiple_of(c * HG_CHUNK, HG_CHUNK)
            rows = pl.ds(r0, HG_CHUNK)
            q = _silu(p_ref[rows, OFF_HQ + ls.start:OFF_HQ + ls.stop])
            bb = b_ref[rows, ls]

            def group_body(g, acc):
                base = pl.multiple_of(r0 + g * SUBLANES, SUBLANES)
                kg = k_ref[pl.ds(base, SUBLANES), ls]
                bg = b_ref[pl.ds(base, SUBLANES), ls]
                for j in range(SUBLANES):
                    e = jnp.exp(jnp.minimum(bb - bg[j:j + 1, :], 0.0))
                    colv = jnp.sum(q * kg[j:j + 1, :] * e, axis=-1, keepdims=True)
                    acc = jnp.where(col == g * SUBLANES + j, colv, acc)
                return acc

            a = lax.fori_loop(0, HG_CHUNK // SUBLANES, group_body, jnp.zeros((HG_CHUNK, LANES), F32))
            amat_ref[rows, ls] = jnp.where(tril, a, 0.0).astype(BF16)
            return carry

        lax.fori_loop(0, T // HG_CHUNK, chunk_body, 0)


def _retention_operands(p_ref, cos_ref, sina_ref, sinb_ref, xi_ref, zeta_ref, ret_ref):
    cos, sina, sinb = cos_ref[...], sina_ref[...], sinb_ref[...]

    def rot(u):
        return u * cos + pltpu.roll(u, 2 * RET_DK - RET_DK // 2, 1) * sina + pltpu.roll(u, RET_DK // 2, 1) * sinb

    for pr in range(N_PAIRS):
        qk = slice(pr * 2 * RET_DK, (pr + 1) * 2 * RET_DK)
        qr = rot(p_ref[:, OFF_RQ + qk.start:OFF_RQ + qk.stop])
        kr = rot(p_ref[:, OFF_RK + qk.start:OFF_RK + qk.stop]) * (RET_DK ** -0.5)
        ret_ref[:, RT_Q + qk.start:RT_Q + qk.stop] = qr.astype(BF16)
        ret_ref[:, RT_K + qk.start:RT_K + qk.stop] = kr.astype(BF16)
        ret_ref[:, RT_KZ + qk.start:RT_KZ + qk.stop] = (kr * zeta_ref[pr]).astype(BF16)
        ret_ref[:, RT_QXI + qk.start:RT_QXI + qk.stop] = (qr * xi_ref[pr]).astype(BF16)
    ret_ref[:, RT_V:RT_V + N_HEADS * HEAD_DV] = p_ref[:, OFF_RV:OFF_RV + N_HEADS * HEAD_DV].astype(BF16)


def _front_kernel(x_ref, mod_ref, ng_ref, win_ref, lb_ref, ltri_ref, cos_ref, sina_ref, sinb_ref,
                  xi_ref, zeta_ref,
                  hg_ref, ret_ref, gate_ref, amat_ref, dcol_ref,
                  p_ref, hb_ref, k_ref, b_ref, lhi_ref, llo_ref):
    _normalize(x_ref[...], mod_ref[...], ng_ref, hb_ref)
    proj = _ProjectionStream(
        hb_ref, win_ref, p_ref,
        _pieces(OFF_HF, 1024) + _pieces(OFF_HZ, 1024) + _pieces(OFF_GA, 1024)
        + _pieces(OFF_RZ, 1024) + _pieces(OFF_GB, 1024) + _pieces(OFF_HQ, 1024) + _pieces(OFF_HI, 1024)
        + _pieces(OFF_RQ, 1024) + _pieces(OFF_RV, 1024))
    proj.emit(4)
    _forget_gates(p_ref, lb_ref, k_ref, lhi_ref, llo_ref, proj.emit)
    ltri = ltri_ref[...]
    b_ref[...] = _dot(ltri, lhi_ref[...]) + _dot(ltri, llo_ref[...])
    _gate_product(p_ref, OFF_GA, OFF_HZ, gate_ref, GT_A, proj.emit)
    _gate_product(p_ref, OFF_GB, OFF_RZ, gate_ref, GT_B, proj.emit)
    _hgrn_operands(p_ref, k_ref, b_ref, hg_ref, amat_ref, dcol_ref, _every(4, proj.emit))
    _retention_operands(p_ref, cos_ref, sina_ref, sinb_ref, xi_ref, zeta_ref, ret_ref)

    worst = jnp.float32(0.0)
    for c in range(x_ref.shape[0] // HG_CHUNK):
        worst = jnp.maximum(worst, jnp.max(-b_ref[(c + 1) * HG_CHUNK - 1:(c + 1) * HG_CHUNK, :]))

    @pl.when(worst > EXP_CLAMP)
    def _():
        _hgrn_scores_exact(p_ref, k_ref, b_ref, amat_ref)


def _hgrn_recurrence(hg_ref, amat_ref, dcol_ref, s_ref, oa_ref):
    T = hg_ref.shape[0]
    states = [s_ref[hd] for hd in range(N_HEADS)]
    for c in range(T // HG_CHUNK):
        rs = slice(c * HG_CHUNK, (c + 1) * HG_CHUNK)
        for hd in range(N_HEADS):
            ls = slice(hd * HG_DK, (hd + 1) * HG_DK)
            qt = hg_ref[rs, HG_QT + ls.start:HG_QT + ls.stop]
            kh = hg_ref[rs, HG_KH + ls.start:HG_KH + ls.stop]
            v = hg_ref[rs, HG_V + ls.start:HG_V + ls.stop]
            st = states[hd]
            intra = _dot(amat_ref[rs, ls.start:ls.start + HG_CHUNK], v)
            oa_ref[rs, ls] = intra + _dot(qt, st.astype(BF16))
            states[hd] = st * dcol_ref[c, ls, :] + lax.dot_general(kh, v, TN_DIMS, preferred_element_type=F32)
    for hd in range(N_HEADS):
        s_ref[hd] = states[hd]


def _retention_recurrence(ret_ref, dm_ref, rdec_ref, r_ref, ob_ref):
    T = ret_ref.shape[0]
    first_head = lax.broadcasted_iota(jnp.int32, (RET_CHUNK, 2 * RET_DK), 1) < RET_DK
    bd_row = lax.broadcasted_iota(jnp.int32, (2 * RET_DK, 2 * HEAD_DV), 0)
    bd_col = lax.broadcasted_iota(jnp.int32, (2 * RET_DK, 2 * HEAD_DV), 1)
    bd_mask = (bd_row < RET_DK) == (bd_col < HEAD_DV)
    states = [r_ref[pr] for pr in range(N_PAIRS)]
    for c in range(T // RET_CHUNK):
        rs = slice(c * RET_CHUNK, (c + 1) * RET_CHUNK)
        for pr in range(N_PAIRS):
            qk = slice(pr * 2 * RET_DK, (pr + 1) * 2 * RET_DK)
            vs = slice(pr * 2 * HEAD_DV, (pr + 1) * 2 * HEAD_DV)
            rbd = states[pr]
            qr = ret_ref[rs, RT_Q + qk.start:RT_Q + qk.stop]
            kr = ret_ref[rs, RT_K + qk.start:RT_K + qk.stop]
            vpair = ret_ref[rs, RT_V + vs.start:RT_V + vs.stop]
            zero = jnp.zeros_like(qr)
            q0 = jnp.where(first_head, qr, zero)
            q1 = jnp.where(first_head, zero, qr)
            sc = lax.dot_general(jnp.concatenate([q0, q1], axis=0), kr, NT_DIMS, preferred_element_type=F32)
            sc0 = sc[:RET_CHUNK] * dm_ref[2 * pr]
            sc1 = sc[RET_CHUNK:] * dm_ref[2 * pr + 1]
            intra0 = _dot(sc0.astype(BF16), vpair[:, :HEAD_DV])
            intra1 = _dot(sc1.astype(BF16), vpair[:, HEAD_DV:])
            cross = _dot(ret_ref[rs, RT_QXI + qk.start:RT_QXI + qk.stop], rbd.astype(BF16))
            ob_ref[rs, vs] = jnp.concatenate([intra0, intra1], axis=1) + cross
            cstate = lax.dot_general(ret_ref[rs, RT_KZ + qk.start:RT_KZ + qk.stop], vpair, TN_DIMS,
                                     preferred_element_type=F32)
            states[pr] = rbd * rdec_ref[pr] + jnp.where(bd_mask, cstate, 0.0)
    for pr in range(N_PAIRS):
        r_ref[pr] = states[pr]


def _merge_out(x, gate, oa_ref, ob_ref, gate_ref, hgg_ref, rtg_ref, fg_ref, wout_ref, m_ref, o_ref):
    def normed(o, g):
        return o * lax.rsqrt(jnp.mean(o * o, axis=-1, keepdims=True) + EPS) * g

    for hd in range(N_HEADS):
        ls = slice(hd * HEAD_DV, (hd + 1) * HEAD_DV)
        ua = normed(oa_ref[:, ls], hgg_ref[:, ls]) * gate_ref[:, GT_A + ls.start:GT_A + ls.stop]
        ub = normed(ob_ref[:, ls], rtg_ref[:, ls]) * gate_ref[:, GT_B + ls.start:GT_B + ls.stop]
        m_ref[:, ls] = (ua + ub).astype(BF16)
    xo = x + gate * _dot(m_ref[...], wout_ref[...])
    o_ref[...] = xo * lax.rsqrt(jnp.mean(xo * xo, axis=-1, keepdims=True) + EPS) * fg_ref[...]


def _back_kernel(x_ref, mod_ref, hg_ref, ret_ref, gate_ref, amat_ref, dcol_ref, hgg_ref, rtg_ref, fg_ref,
                 wout_ref, dm_ref, rdec_ref,
                 o_ref,
                 s_ref, r_ref, oa_ref, ob_ref, m_ref):
    @pl.when(pl.program_id(1) == 0)
    def _():
        s_ref[...] = jnp.zeros_like(s_ref)
        r_ref[...] = jnp.zeros_like(r_ref)

    _hgrn_recurrence(hg_ref, amat_ref, dcol_ref, s_ref, oa_ref)
    _retention_recurrence(ret_ref, dm_ref, rdec_ref, r_ref, ob_ref)
    _merge_out(x_ref[...], mod_ref[:, 2 * D_MODEL:], oa_ref, ob_ref, gate_ref, hgg_ref, rtg_ref, fg_ref,
               wout_ref, m_ref, o_ref)


def _rotary_tables(seq_len):
    half = RET_DK // 2
    inv_freq = 1.0 / (ROPE_BASE ** jnp.linspace(0.0, 1.0, half, dtype=F32))
    ang = jnp.arange(seq_len, dtype=jnp.int32).astype(F32)[:, None] * inv_freq[None, :]
    cos, sin = jnp.cos(ang), jnp.sin(ang)
    zero = jnp.zeros_like(sin)
    cos_t = jnp.tile(cos, (1, 4))
    sina_t = jnp.tile(jnp.concatenate([-sin, zero], axis=1), (1, 2))
    sinb_t = jnp.tile(jnp.concatenate([zero, sin], axis=1), (1, 2))
    return cos_t, sina_t, sinb_t


def _retention_tables(tile_t):
    c = RET_CHUNK
    log_gamma = jnp.log(1.0 - jnp.exp2(-5.0 - jnp.arange(N_HEADS, dtype=F32)))
    idx = jnp.arange(c, dtype=F32)
    rel = idx[:, None] - idx[None, :]
    dm = jnp.where(rel >= 0, jnp.exp(log_gamma[:, None, None] * jnp.maximum(rel, 0.0)), 0.0)
    zeta = jnp.exp(log_gamma[:, None] * (c - 1.0 - idx))
    xi = jnp.exp(log_gamma[:, None] * (idx + 1.0))
    decay = jnp.exp(log_gamma * c)

    def per_pair_lanes(t):
        t = jnp.transpose(t.reshape(N_PAIRS, 2, c), (0, 2, 1))
        return jnp.tile(jnp.repeat(t, RET_DK, axis=2), (1, tile_t // c, 1))

    dec = jnp.repeat(decay.reshape(N_PAIRS, 1, 2), HEAD_DV, axis=2)
    return dm, per_pair_lanes(xi), per_pair_lanes(zeta), dec


def _chunk_tril(tile_t):
    r = jnp.arange(tile_t, dtype=jnp.int32)
    same_chunk = (r[:, None] // HG_CHUNK) == (r[None, :] // HG_CHUNK)
    return jnp.where(same_chunk & (r[:, None] >= r[None, :]), 1.0, 0.0).astype(BF16)


def _const_spec(shape, single_buffer=False):
    zeros = (0,) * len(shape)
    if single_buffer:
        return pl.BlockSpec(shape, lambda b, t: zeros, pipeline_mode=pl.Buffered(1))
    return pl.BlockSpec(shape, lambda b, t: zeros)


def _prep_call(c, w_ada, b_ada, lb_logits):
    bsz = c.shape[0]
    return pl.pallas_call(
        _prep_kernel,
        out_shape=(jax.ShapeDtypeStruct((bsz, 3 * D_MODEL), F32),
                   jax.ShapeDtypeStruct((1, N_HEADS * HG_DK), F32)),
        compiler_params=pltpu.CompilerParams(vmem_limit_bytes=V7X_VMEM_LIMIT_BYTES),
        name="adaln_prep",
    )(c, w_ada, b_ada.reshape(1, -1), lb_logits)


def _tile_spec(tile_t, width):
    return pl.BlockSpec((None, tile_t, width), lambda b, t: (b, t, 0))


def _front_call(x, mod3, norm_g, w_in, lb, tile_t):
    bsz, seq, d = x.shape
    n_hc = tile_t // HG_CHUNK
    cos_t, sina_t, sinb_t = _rotary_tables(seq)
    _, xi, zeta, _ = _retention_tables(tile_t)
    ltri = _chunk_tril(tile_t)
    rot_spec = pl.BlockSpec((tile_t, 2 * RET_DK), lambda b, t: (t, 0))
    dcol_spec = pl.BlockSpec((None, n_hc, d, HEAD_DV), lambda b, t: (b, t, 0, 0))
    bf = lambda width: jax.ShapeDtypeStruct((bsz, seq, width), BF16)
    return pl.pallas_call(
        _front_kernel,
        grid=(bsz, seq // tile_t),
        in_specs=[
            _tile_spec(tile_t, d),
            pl.BlockSpec((None, 1, 3 * d), lambda b, t: (b, 0, 0)),
            _const_spec((1, d)),
            _const_spec((d, D_IN), single_buffer=True),
            _const_spec((1, d)),
            _const_spec(ltri.shape),
            rot_spec, rot_spec, rot_spec,
            _const_spec(xi.shape), _const_spec(zeta.shape),
        ],
        out_specs=[
            _tile_spec(tile_t, HG_WIDTH), _tile_spec(tile_t, RT_WIDTH), _tile_spec(tile_t, GT_WIDTH),
            _tile_spec(tile_t, d), dcol_spec,
        ],
        out_shape=[bf(HG_WIDTH), bf(RT_WIDTH), bf(GT_WIDTH), bf(d),
                   jax.ShapeDtypeStruct((bsz, seq // HG_CHUNK, d, HEAD_DV), F32)],
        scratch_shapes=[
            pltpu.VMEM((tile_t, D_IN), F32),
            pltpu.VMEM((tile_t, d), BF16),
            pltpu.VMEM((tile_t, d), F32),
            pltpu.VMEM((tile_t, d), F32),
            pltpu.VMEM((tile_t, d), BF16),
            pltpu.VMEM((tile_t, d), BF16),
        ],
        compiler_params=pltpu.CompilerParams(
            dimension_semantics=("arbitrary", "arbitrary"),
            vmem_limit_bytes=V7X_VMEM_LIMIT_BYTES),
        name="layer_front",
    )(x, mod3, norm_g.reshape(1, -1), w_in, lb, ltri, cos_t, sina_t, sinb_t, xi, zeta)


def _back_call(x, mod3, hg, ret, gate, amat, dcol, hg_g, ret_g, final_g, w_out, tile_t):
    bsz, seq, d = x.shape
    n_hc = tile_t // HG_CHUNK
    dm, _, _, rdec = _retention_tables(tile_t)
    row = lambda v: v.reshape(1, -1)
    return pl.pallas_call(
        _back_kernel,
        grid=(bsz, seq // tile_t),
        in_specs=[
            _tile_spec(tile_t, d),
            pl.BlockSpec((None, 1, 3 * d), lambda b, t: (b, 0, 0)),
            _tile_spec(tile_t, HG_WIDTH), _tile_spec(tile_t, RT_WIDTH), _tile_spec(tile_t, GT_WIDTH),
            _tile_spec(tile_t, d),
            pl.BlockSpec((None, n_hc, d, HEAD_DV), lambda b, t: (b, t, 0, 0)),
            _const_spec((1, d)), _const_spec((1, d)), _const_spec((1, d)),
            _const_spec((d, d), single_buffer=True),
            _const_spec(dm.shape), _const_spec(rdec.shape),
        ],
        out_specs=_tile_spec(tile_t, d),
        out_shape=jax.ShapeDtypeStruct(x.shape, x.dtype),
        scratch_shapes=[
            pltpu.VMEM((N_HEADS, HG_DK, HEAD_DV), F32),
            pltpu.VMEM((N_PAIRS, 2 * RET_DK, 2 * HEAD_DV), F32),
            pltpu.VMEM((tile_t, d), F32),
            pltpu.VMEM((tile_t, d), F32),
            pltpu.VMEM((tile_t, d), BF16),
        ],
        compiler_params=pltpu.CompilerParams(
            dimension_semantics=("arbitrary", "arbitrary"),
            vmem_limit_bytes=V7X_VMEM_LIMIT_BYTES),
        name="layer_back",
    )(x, mod3, hg, ret, gate, amat, dcol, row(hg_g), row(ret_g), row(final_g), w_out, dm, rdec)


def kernel(x, c, norm_g, w_ada, b_ada, w_in, hg_lb_logits, hg_norm_g, ret_norm_g, w_out, final_g):
    depth = norm_g.shape[0]
    assert depth == 1 and hg_lb_logits.shape[0] == 2
    bsz, seq, d = x.shape
    tile_t = min(TILE_T, seq)
    assert d == D_MODEL and seq % tile_t == 0 and tile_t % RET_CHUNK == 0
    mod, lb = _prep_call(c, w_ada[0], b_ada[0], hg_lb_logits)
    mod3 = mod.reshape(bsz, 1, 3 * d)
    hg, ret, gate, amat, dcol = _front_call(x, mod3, norm_g[0], w_in[0].astype(BF16), lb, tile_t)
    return _back_call(x, mod3, hg, ret, gate, amat, dcol, hg_norm_g[0], ret_norm_g[0], final_g,
                      w_out[0].astype(BF16), tile_t)
```

```python
import jax
import jax.numpy as jnp
from jax import lax
from jax.experimental import pallas as pl
from jax.experimental.pallas import tpu as pltpu

F32 = jnp.float32
BF16 = jnp.bfloat16

D_MODEL = 1024
N_HEADS = 8
HEAD_DV = 128
HG_DK = 128
RET_DK = 64
ROPE_BASE = 10000.0
EPS = 1e-6

TILE_T = 256
TILE_BACK = 512
HG_CHUNK = 64
RET_CHUNK = 128
N_PAIRS = N_HEADS // 2
SUBLANES = 8
LANES = 128

OFF_HQ, OFF_HF, OFF_HI, OFF_HZ = 0, 1024, 2048, 3072
OFF_RQ, OFF_RK, OFF_RV, OFF_RZ = 4096, 4608, 5120, 6144
OFF_GA, OFF_GB = 7168, 8192
D_IN = 9216
PROJ_BLOCK = 256

HG_QT, HG_KH, HG_V, HG_WIDTH = 0, 1024, 2048, 3072
RT_Q, RT_K, RT_KZ, RT_QXI, RT_V, RT_WIDTH = 0, 512, 1024, 1536, 2048, 3072
GT_A, GT_B, GT_WIDTH = 0, 1024, 2048

EXP_CLAMP = 80.0

V7X_VMEM_LIMIT_BYTES = 58 * 1024 * 1024

NT_DIMS = (((1,), (1,)), ((), ()))
TN_DIMS = (((0,), (0,)), ((), ()))


def _sigmoid(x):
    return 0.5 * (jnp.tanh(0.5 * x) + 1.0)


def _silu(x):
    return x * _sigmoid(x)


def _dot(a, b):
    return jnp.dot(a, b, preferred_element_type=F32)


def _prep_kernel(c_ref, wada_ref, bada_ref, lbl_ref, mod_ref, lb_ref):
    c = c_ref[...]
    sc = _silu(c).astype(BF16)
    mod_ref[...] = _dot(sc, wada_ref[...].astype(BF16)) + bada_ref[...]
    lg = lbl_ref[...]
    l0, l1 = lg[0:1, :], lg[1:2, :]
    mx = jnp.maximum(l0, l1)
    e0, e1 = jnp.exp(l0 - mx), jnp.exp(l1 - mx)
    lb_ref[...] = e0 / (e0 + e1)


class _ProjectionStream:
    def __init__(self, hb_ref, win_ref, p_ref, pieces):
        self.hb_ref, self.win_ref, self.p_ref = hb_ref, win_ref, p_ref
        self.pieces = list(pieces)

    def emit(self, n_pieces=1):
        for _ in range(min(n_pieces, len(self.pieces))):
            cs = slice(self.pieces[0] * PROJ_BLOCK, (self.pieces[0] + 1) * PROJ_BLOCK)
            self.pieces.pop(0)
            self.p_ref[:, cs] = _dot(self.hb_ref[...], self.win_ref[:, cs])


def _pieces(offset, width):
    return list(range(offset // PROJ_BLOCK, (offset + width) // PROJ_BLOCK))


def _every(n_calls, fn):
    count = [0]

    def tick():
        count[0] += 1
        if count[0] % n_calls == 0:
            fn()

    return tick


def _normalize(x, mod, ng_ref, hb_ref):
    shift, scale = mod[:, :D_MODEL], mod[:, D_MODEL:2 * D_MODEL]
    ms = jnp.mean(x * x, axis=-1, keepdims=True)
    h = x * lax.rsqrt(ms + EPS) * ng_ref[...] * (1.0 + scale) + shift
    hb_ref[...] = h.astype(BF16)


def _forget_gates(p_ref, lb_ref, k_ref, lhi_ref, llo_ref, tick):
    for hd in range(N_HEADS):
        ls = slice(hd * HG_DK, (hd + 1) * HG_DK)
        lbh = lb_ref[:, ls]
        f = lbh + (1.0 - lbh) * _sigmoid(p_ref[:, OFF_HF + ls.start:OFF_HF + ls.stop])
        k_ref[:, ls] = 1.0 - f
        logf = jnp.log(f)
        hi = logf.astype(BF16)
        lhi_ref[:, ls] = hi
        llo_ref[:, ls] = (logf - hi.astype(F32)).astype(BF16)
        tick()


def _gate_product(p_ref, g_off, z_off, gate_ref, out_off, tick):
    for hd in range(N_HEADS):
        ls = slice(hd * HEAD_DV, (hd + 1) * HEAD_DV)
        g = _sigmoid(p_ref[:, g_off + ls.start:g_off + ls.stop])
        z = _silu(p_ref[:, z_off + ls.start:z_off + ls.stop])
        gate_ref[:, out_off + ls.start:out_off + ls.stop] = (g * z).astype(BF16)
        tick()


def _padded_tril():
    row = lax.broadcasted_iota(jnp.int32, (HG_CHUNK, LANES), 0)
    col = lax.broadcasted_iota(jnp.int32, (HG_CHUNK, LANES), 1)
    return row >= col, col


def _hgrn_operands(p_ref, k_ref, b_ref, hg_ref, amat_ref, dec_ref, tick):
    T = p_ref.shape[0]
    tril = (lax.broadcasted_iota(jnp.int32, (HG_CHUNK, HG_CHUNK), 0)
            >= lax.broadcasted_iota(jnp.int32, (HG_CHUNK, HG_CHUNK), 1))
    amat_ref[...] = jnp.zeros_like(amat_ref)
    for c in range(T // HG_CHUNK):
        rs = slice(c * HG_CHUNK, (c + 1) * HG_CHUNK)
        for hd in range(N_HEADS):
            ls = slice(hd * HG_DK, (hd + 1) * HG_DK)
            q = _silu(p_ref[rs, OFF_HQ + ls.start:OFF_HQ + ls.stop])
            bb = b_ref[rs, ls]
            blast = bb[HG_CHUNK - 1:HG_CHUNK, :]
            qt = q * jnp.exp(bb)
            kh = (k_ref[rs, ls] * jnp.exp(blast - bb)).astype(BF16)
            hg_ref[rs, HG_QT + ls.start:HG_QT + ls.stop] = qt.astype(BF16)
            hg_ref[rs, HG_KH + ls.start:HG_KH + ls.stop] = kh
            hg_ref[rs, HG_V + ls.start:HG_V + ls.stop] = p_ref[rs, OFF_HI + ls.start:OFF_HI + ls.stop].astype(BF16)
            dec_ref[c, :, ls] = jnp.exp(blast)
            qh = (qt * jnp.exp(jnp.minimum(-blast, EXP_CLAMP))).astype(BF16)
            a = lax.dot_general(qh, kh, NT_DIMS, preferred_element_type=F32)
            amat_ref[rs, ls.start:ls.start + HG_CHUNK] = jnp.where(tril, a, 0.0).astype(BF16)
            tick()


def _hgrn_scores_exact(p_ref, k_ref, b_ref, amat_ref):
    T = p_ref.shape[0]
    tril, col = _padded_tril()
    for hd in range(N_HEADS):
        ls = slice(hd * HG_DK, (hd + 1) * HG_DK)

        def chunk_body(c, carry):
            r0 = pl.multiple_of(c * HG_CHUNK, HG_CHUNK)
            rows = pl.ds(r0, HG_CHUNK)
            q = _silu(p_ref[rows, OFF_HQ + ls.start:OFF_HQ + ls.stop])
            bb = b_ref[rows, ls]

            def group_body(g, acc):
                base = pl.multiple_of(r0 + g * SUBLANES, SUBLANES)
                kg = k_ref[pl.ds(base, SUBLANES), ls]
                bg = b_ref[pl.ds(base, SUBLANES), ls]
                for j in range(SUBLANES):
                    e = jnp.exp(jnp.minimum(bb - bg[j:j + 1, :], 0.0))
                    colv = jnp.sum(q * kg[j:j + 1, :] * e, axis=-1, keepdims=True)
                    acc = jnp.where(col == g * SUBLANES + j, colv, acc)
                return acc

            a = lax.fori_loop(0, HG_CHUNK // SUBLANES, group_body, jnp.zeros((HG_CHUNK, LANES), F32))
            amat_ref[rows, ls] = jnp.where(tril, a, 0.0).astype(BF16)
            return carry

        lax.fori_loop(0, T // HG_CHUNK, chunk_body, 0)


def _retention_operands(p_ref, cos_ref, sina_ref, sinb_ref, xi_ref, zeta_ref, ret_ref):
    cos, sina, sinb = cos_ref[...], sina_ref[...], sinb_ref[...]

    def rot(u):
        return u * cos + pltpu.roll(u, 2 * RET_DK - RET_DK // 2, 1) * sina + pltpu.roll(u, RET_DK // 2, 1) * sinb

    for pr in range(N_PAIRS):
        qk = slice(pr * 2 * RET_DK, (pr + 1) * 2 * RET_DK)
        qr = rot(p_ref[:, OFF_RQ + qk.start:OFF_RQ + qk.stop])
        kr = rot(p_ref[:, OFF_RK + qk.start:OFF_RK + qk.stop]) * (RET_DK ** -0.5)
        ret_ref[:, RT_Q + qk.start:RT_Q + qk.stop] = qr.astype(BF16)
        ret_ref[:, RT_K + qk.start:RT_K + qk.stop] = kr.astype(BF16)
        ret_ref[:, RT_KZ + qk.start:RT_KZ + qk.stop] = (kr * zeta_ref[pr]).astype(BF16)
        ret_ref[:, RT_QXI + qk.start:RT_QXI + qk.stop] = (qr * xi_ref[pr]).astype(BF16)
    ret_ref[:, RT_V:RT_V + N_HEADS * HEAD_DV] = p_ref[:, OFF_RV:OFF_RV + N_HEADS * HEAD_DV].astype(BF16)


def _front_kernel(x_ref, mod_ref, ng_ref, win_ref, lb_ref, ltri_ref, cos_ref, sina_ref, sinb_ref,
                  xi_ref, zeta_ref,
                  hg_ref, ret_ref, gate_ref, amat_ref, dec_ref,
                  p_ref, hb_ref, k_ref, b_ref, lhi_ref, llo_ref):
    _normalize(x_ref[...], mod_ref[...], ng_ref, hb_ref)
    proj = _ProjectionStream(
        hb_ref, win_ref, p_ref,
        _pieces(OFF_HF, 1024) + _pieces(OFF_HZ, 1024) + _pieces(OFF_GA, 1024)
        + _pieces(OFF_RZ, 1024) + _pieces(OFF_GB, 1024) + _pieces(OFF_HQ, 1024) + _pieces(OFF_HI, 1024)
        + _pieces(OFF_RQ, 1024) + _pieces(OFF_RV, 1024))
    proj.emit(4)
    _forget_gates(p_ref, lb_ref, k_ref, lhi_ref, llo_ref, proj.emit)
    ltri = ltri_ref[...]
    b_ref[...] = _dot(ltri, lhi_ref[...]) + _dot(ltri, llo_ref[...])
    _gate_product(p_ref, OFF_GA, OFF_HZ, gate_ref, GT_A, proj.emit)
    _gate_product(p_ref, OFF_GB, OFF_RZ, gate_ref, GT_B, proj.emit)
    _hgrn_operands(p_ref, k_ref, b_ref, hg_ref, amat_ref, dec_ref, _every(4, proj.emit))
    _retention_operands(p_ref, cos_ref, sina_ref, sinb_ref, xi_ref, zeta_ref, ret_ref)

    worst = jnp.float32(0.0)
    for c in range(x_ref.shape[0] // HG_CHUNK):
        worst = jnp.maximum(worst, jnp.max(-b_ref[(c + 1) * HG_CHUNK - 1:(c + 1) * HG_CHUNK, :]))

    @pl.when(worst > EXP_CLAMP)
    def _():
        _hgrn_scores_exact(p_ref, k_ref, b_ref, amat_ref)


def _hgrn_recurrence(hg_ref, amat_ref, dec_ref, s_ref, oa_ref):
    T = hg_ref.shape[0]
    states = [s_ref[hd] for hd in range(N_HEADS)]
    for c in range(T // HG_CHUNK):
        rs = slice(c * HG_CHUNK, (c + 1) * HG_CHUNK)
        for hd in range(N_HEADS):
            ls = slice(hd * HG_DK, (hd + 1) * HG_DK)
            qt = hg_ref[rs, HG_QT + ls.start:HG_QT + ls.stop]
            kh = hg_ref[rs, HG_KH + ls.start:HG_KH + ls.stop]
            v = hg_ref[rs, HG_V + ls.start:HG_V + ls.stop]
            st = states[hd]
            intra = _dot(amat_ref[rs, ls.start:ls.start + HG_CHUNK], v)
            oa_ref[rs, ls] = intra + _dot(qt, st.astype(BF16))
            decay = jnp.transpose(jnp.broadcast_to(dec_ref[c, :, ls], (HEAD_DV, HG_DK)))
            states[hd] = st * decay + lax.dot_general(kh, v, TN_DIMS, preferred_element_type=F32)
    for hd in range(N_HEADS):
        s_ref[hd] = states[hd]


def _retention_recurrence(ret_ref, dm_ref, rdec_ref, r_ref, ob_ref):
    T = ret_ref.shape[0]
    first_head = lax.broadcasted_iota(jnp.int32, (RET_CHUNK, 2 * RET_DK), 1) < RET_DK
    bd_row = lax.broadcasted_iota(jnp.int32, (2 * RET_DK, 2 * HEAD_DV), 0)
    bd_col = lax.broadcasted_iota(jnp.int32, (2 * RET_DK, 2 * HEAD_DV), 1)
    bd_mask = (bd_row < RET_DK) == (bd_col < HEAD_DV)
    states = [r_ref[pr] for pr in range(N_PAIRS)]
    for c in range(T // RET_CHUNK):
        rs = slice(c * RET_CHUNK, (c + 1) * RET_CHUNK)
        for pr in range(N_PAIRS):
            qk = slice(pr * 2 * RET_DK, (pr + 1) * 2 * RET_DK)
            vs = slice(pr * 2 * HEAD_DV, (pr + 1) * 2 * HEAD_DV)
            rbd = states[pr]
            qr = ret_ref[rs, RT_Q + qk.start:RT_Q + qk.stop]
            kr = ret_ref[rs, RT_K + qk.start:RT_K + qk.stop]
            vpair = ret_ref[rs, RT_V + vs.start:RT_V + vs.stop]
            zero = jnp.zeros_like(qr)
            q0 = jnp.where(first_head, qr, zero)
            q1 = jnp.where(first_head, zero, qr)
            sc = lax.dot_general(jnp.concatenate([q0, q1], axis=0), kr, NT_DIMS, preferred_element_type=F32)
            sc0 = sc[:RET_CHUNK] * dm_ref[2 * pr]
            sc1 = sc[RET_CHUNK:] * dm_ref[2 * pr + 1]
            intra0 = _dot(sc0.astype(BF16), vpair[:, :HEAD_DV])
            intra1 = _dot(sc1.astype(BF16), vpair[:, HEAD_DV:])
            cross = _dot(ret_ref[rs, RT_QXI + qk.start:RT_QXI + qk.stop], rbd.astype(BF16))
            ob_ref[rs, vs] = jnp.concatenate([intra0, intra1], axis=1) + cross
            cstate = lax.dot_general(ret_ref[rs, RT_KZ + qk.start:RT_KZ + qk.stop], vpair, TN_DIMS,
                                     preferred_element_type=F32)
            states[pr] = rbd * rdec_ref[pr] + jnp.where(bd_mask, cstate, 0.0)
    for pr in range(N_PAIRS):
        r_ref[pr] = states[pr]


def _merge_out(x, gate, oa_ref, ob_ref, gate_ref, hgg_ref, rtg_ref, fg_ref, wout_ref, m_ref, o_ref):
    def normed(o, g):
        return o * lax.rsqrt(jnp.mean(o * o, axis=-1, keepdims=True) + EPS) * g

    for hd in range(N_HEADS):
        ls = slice(hd * HEAD_DV, (hd + 1) * HEAD_DV)
        ua = normed(oa_ref[:, ls], hgg_ref[:, ls]) * gate_ref[:, GT_A + ls.start:GT_A + ls.stop]
        ub = normed(ob_ref[:, ls], rtg_ref[:, ls]) * gate_ref[:, GT_B + ls.start:GT_B + ls.stop]
        m_ref[:, ls] = (ua + ub).astype(BF16)
    xo = x + gate * _dot(m_ref[...], wout_ref[...])
    o_ref[...] = xo * lax.rsqrt(jnp.mean(xo * xo, axis=-1, keepdims=True) + EPS) * fg_ref[...]


def _back_kernel(x_ref, mod_ref, hg_ref, ret_ref, gate_ref, amat_ref, dec_ref, hgg_ref, rtg_ref, fg_ref,
                 wout_ref, dm_ref, rdec_ref,
                 o_ref,
                 s_ref, r_ref, oa_ref, ob_ref, m_ref):
    @pl.when(pl.program_id(1) == 0)
    def _():
        s_ref[...] = jnp.zeros_like(s_ref)
        r_ref[...] = jnp.zeros_like(r_ref)

    _hgrn_recurrence(hg_ref, amat_ref, dec_ref, s_ref, oa_ref)
    _retention_recurrence(ret_ref, dm_ref, rdec_ref, r_ref, ob_ref)
    _merge_out(x_ref[...], mod_ref[:, 2 * D_MODEL:], oa_ref, ob_ref, gate_ref, hgg_ref, rtg_ref, fg_ref,
               wout_ref, m_ref, o_ref)


def _rotary_tables(seq_len):
    half = RET_DK // 2
    inv_freq = 1.0 / (ROPE_BASE ** jnp.linspace(0.0, 1.0, half, dtype=F32))
    ang = jnp.arange(seq_len, dtype=jnp.int32).astype(F32)[:, None] * inv_freq[None, :]
    cos, sin = jnp.cos(ang), jnp.sin(ang)
    zero = jnp.zeros_like(sin)
    cos_t = jnp.tile(cos, (1, 4))
    sina_t = jnp.tile(jnp.concatenate([-sin, zero], axis=1), (1, 2))
    sinb_t = jnp.tile(jnp.concatenate([zero, sin], axis=1), (1, 2))
    return cos_t, sina_t, sinb_t


def _retention_tables(tile_t):
    c = RET_CHUNK
    log_gamma = jnp.log(1.0 - jnp.exp2(-5.0 - jnp.arange(N_HEADS, dtype=F32)))
    idx = jnp.arange(c, dtype=F32)
    rel = idx[:, None] - idx[None, :]
    dm = jnp.where(rel >= 0, jnp.exp(log_gamma[:, None, None] * jnp.maximum(rel, 0.0)), 0.0)
    zeta = jnp.exp(log_gamma[:, None] * (c - 1.0 - idx))
    xi = jnp.exp(log_gamma[:, None] * (idx + 1.0))
    decay = jnp.exp(log_gamma * c)

    def per_pair_lanes(t):
        t = jnp.transpose(t.reshape(N_PAIRS, 2, c), (0, 2, 1))
        return jnp.tile(jnp.repeat(t, RET_DK, axis=2), (1, tile_t // c, 1))

    dec = jnp.repeat(decay.reshape(N_PAIRS, 1, 2), HEAD_DV, axis=2)
    return dm, per_pair_lanes(xi), per_pair_lanes(zeta), dec


def _chunk_tril(tile_t):
    r = jnp.arange(tile_t, dtype=jnp.int32)
    same_chunk = (r[:, None] // HG_CHUNK) == (r[None, :] // HG_CHUNK)
    return jnp.where(same_chunk & (r[:, None] >= r[None, :]), 1.0, 0.0).astype(BF16)


def _const_spec(shape, single_buffer=False):
    zeros = (0,) * len(shape)
    if single_buffer:
        return pl.BlockSpec(shape, lambda b, t: zeros, pipeline_mode=pl.Buffered(1))
    return pl.BlockSpec(shape, lambda b, t: zeros)


def _prep_call(c, w_ada, b_ada, lb_logits):
    bsz = c.shape[0]
    return pl.pallas_call(
        _prep_kernel,
        out_shape=(jax.ShapeDtypeStruct((bsz, 3 * D_MODEL), F32),
                   jax.ShapeDtypeStruct((1, N_HEADS * HG_DK), F32)),
        compiler_params=pltpu.CompilerParams(vmem_limit_bytes=V7X_VMEM_LIMIT_BYTES),
        name="adaln_prep",
    )(c, w_ada, b_ada.reshape(1, -1), lb_logits)


def _tile_spec(tile_t, width):
    return pl.BlockSpec((None, tile_t, width), lambda b, t: (b, t, 0))


def _front_call(x, mod3, norm_g, w_in, lb, tile_t):
    bsz, seq, d = x.shape
    n_hc = tile_t // HG_CHUNK
    cos_t, sina_t, sinb_t = _rotary_tables(seq)
    _, xi, zeta, _ = _retention_tables(tile_t)
    ltri = _chunk_tril(tile_t)
    rot_spec = pl.BlockSpec((tile_t, 2 * RET_DK), lambda b, t: (t, 0))
    dec_spec = pl.BlockSpec((None, n_hc, 1, d), lambda b, t: (b, t, 0, 0))
    bf = lambda width: jax.ShapeDtypeStruct((bsz, seq, width), BF16)
    return pl.pallas_call(
        _front_kernel,
        grid=(bsz, seq // tile_t),
        in_specs=[
            _tile_spec(tile_t, d),
            pl.BlockSpec((None, 1, 3 * d), lambda b, t: (b, 0, 0)),
            _const_spec((1, d)),
            _const_spec((d, D_IN), single_buffer=True),
            _const_spec((1, d)),
            _const_spec(ltri.shape),
            rot_spec, rot_spec, rot_spec,
            _const_spec(xi.shape), _const_spec(zeta.shape),
        ],
        out_specs=[
            _tile_spec(tile_t, HG_WIDTH), _tile_spec(tile_t, RT_WIDTH), _tile_spec(tile_t, GT_WIDTH),
            _tile_spec(tile_t, d), dec_spec,
        ],
        out_shape=[bf(HG_WIDTH), bf(RT_WIDTH), bf(GT_WIDTH), bf(d),
                   jax.ShapeDtypeStruct((bsz, seq // HG_CHUNK, 1, d), F32)],
        scratch_shapes=[
            pltpu.VMEM((tile_t, D_IN), F32),
            pltpu.VMEM((tile_t, d), BF16),
            pltpu.VMEM((tile_t, d), F32),
            pltpu.VMEM((tile_t, d), F32),
            pltpu.VMEM((tile_t, d), BF16),
            pltpu.VMEM((tile_t, d), BF16),
        ],
        compiler_params=pltpu.CompilerParams(
            dimension_semantics=("arbitrary", "arbitrary"),
            vmem_limit_bytes=V7X_VMEM_LIMIT_BYTES),
        name="layer_front",
    )(x, mod3, norm_g.reshape(1, -1), w_in, lb, ltri, cos_t, sina_t, sinb_t, xi, zeta)


def _back_call(x, mod3, hg, ret, gate, amat, dec, hg_g, ret_g, final_g, w_out, tile_t):
    bsz, seq, d = x.shape
    n_hc = tile_t // HG_CHUNK
    dm, _, _, rdec = _retention_tables(tile_t)
    row = lambda v: v.reshape(1, -1)
    return pl.pallas_call(
        _back_kernel,
        grid=(bsz, seq // tile_t),
        in_specs=[
            _tile_spec(tile_t, d),
            pl.BlockSpec((None, 1, 3 * d), lambda b, t: (b, 0, 0)),
            _tile_spec(tile_t, HG_WIDTH), _tile_spec(tile_t, RT_WIDTH), _tile_spec(tile_t, GT_WIDTH),
            _tile_spec(tile_t, d),
            pl.BlockSpec((None, n_hc, 1, d), lambda b, t: (b, t, 0, 0)),
            _const_spec((1, d)), _const_spec((1, d)), _const_spec((1, d)),
            _const_spec((d, d), single_buffer=True),
            _const_spec(dm.shape), _const_spec(rdec.shape),
        ],
        out_specs=_tile_spec(tile_t, d),
        out_shape=jax.ShapeDtypeStruct(x.shape, x.dtype),
        scratch_shapes=[
            pltpu.VMEM((N_HEADS, HG_DK, HEAD_DV), F32),
            pltpu.VMEM((N_PAIRS, 2 * RET_DK, 2 * HEAD_DV), F32),
            pltpu.VMEM((tile_t, d), F32),
            pltpu.VMEM((tile_t, d), F32),
            pltpu.VMEM((tile_t, d), BF16),
        ],
        compiler_params=pltpu.CompilerParams(
            dimension_semantics=("arbitrary", "arbitrary"),
            vmem_limit_bytes=V7X_VMEM_LIMIT_BYTES),
        name="layer_back",
    )(x, mod3, hg, ret, gate, amat, dec, row(hg_g), row(ret_g), row(final_g), w_out, dm, rdec)


def kernel(x, c, norm_g, w_ada, b_ada, w_in, hg_lb_logits, hg_norm_g, ret_norm_g, w_out, final_g):
    depth = norm_g.shape[0]
    assert depth == 1 and hg_lb_logits.shape[0] == 2
    bsz, seq, d = x.shape
    tile_t = min(TILE_T, seq)
    assert d == D_MODEL and seq % tile_t == 0 and tile_t % RET_CHUNK == 0
    mod, lb = _prep_call(c, w_ada[0], b_ada[0], hg_lb_logits)
    mod3 = mod.reshape(bsz, 1, 3 * d)
    hg, ret, gate, amat, dec = _front_call(x, mod3, norm_g[0], w_in[0].astype(BF16), lb, tile_t)
    return _back_call(x, mod3, hg, ret, gate, amat, dec, hg_norm_g[0], ret_norm_g[0], final_g,
                      w_out[0].astype(BF16), min(TILE_BACK, seq))
```

```python
import jax
import jax.numpy as jnp
from jax import lax
from jax.experimental import pallas as pl
from jax.experimental.pallas import tpu as pltpu

F32 = jnp.float32
BF16 = jnp.bfloat16

D_MODEL = 1024
N_HEADS = 8
HEAD_DV = 128
HG_DK = 128
RET_DK = 64
ROPE_BASE = 10000.0
EPS = 1e-6

TILE_T = 256
TILE_BACK = 512
HG_CHUNK = 64
RET_CHUNK = 128
N_PAIRS = N_HEADS // 2
SUBLANES = 8
LANES = 128

OFF_HQ, OFF_HF, OFF_HI, OFF_HZ = 0, 1024, 2048, 3072
OFF_RQ, OFF_RK, OFF_RV, OFF_RZ = 4096, 4608, 5120, 6144
OFF_GA, OFF_GB = 7168, 8192
D_IN = 9216
PROJ_BLOCK = 256

HG_QT, HG_V, HG_WIDTH = 0, 1024, 2048
RT_Q, RT_K, RT_KZ, RT_QXI, RT_V, RT_WIDTH = 0, 512, 1024, 1536, 2048, 3072
GT_A, GT_B, GT_WIDTH = 0, 1024, 2048

EXP_CLAMP = 80.0

V7X_VMEM_LIMIT_BYTES = 58 * 1024 * 1024

NT_DIMS = (((1,), (1,)), ((), ()))
TN_DIMS = (((0,), (0,)), ((), ()))


def _sigmoid(x):
    return 0.5 * (jnp.tanh(0.5 * x) + 1.0)


def _silu(x):
    return x * _sigmoid(x)


def _dot(a, b):
    return jnp.dot(a, b, preferred_element_type=F32)


def _prep_kernel(c_ref, wada_ref, bada_ref, lbl_ref, mod_ref, lb_ref):
    c = c_ref[...]
    sc = _silu(c).astype(BF16)
    mod_ref[...] = _dot(sc, wada_ref[...].astype(BF16)) + bada_ref[...]
    lg = lbl_ref[...]
    l0, l1 = lg[0:1, :], lg[1:2, :]
    mx = jnp.maximum(l0, l1)
    e0, e1 = jnp.exp(l0 - mx), jnp.exp(l1 - mx)
    lb_ref[...] = e0 / (e0 + e1)


class _ProjectionStream:
    def __init__(self, hb_ref, win_ref, p_ref, pieces):
        self.hb_ref, self.win_ref, self.p_ref = hb_ref, win_ref, p_ref
        self.pieces = list(pieces)

    def emit(self, n_pieces=1):
        for _ in range(min(n_pieces, len(self.pieces))):
            cs = slice(self.pieces[0] * PROJ_BLOCK, (self.pieces[0] + 1) * PROJ_BLOCK)
            self.pieces.pop(0)
            self.p_ref[:, cs] = _dot(self.hb_ref[...], self.win_ref[:, cs])


def _pieces(offset, width):
    return list(range(offset // PROJ_BLOCK, (offset + width) // PROJ_BLOCK))


def _every(n_calls, fn):
    count = [0]

    def tick():
        count[0] += 1
        if count[0] % n_calls == 0:
            fn()

    return tick


def _normalize(x, mod, ng_ref, hb_ref):
    shift, scale = mod[:, :D_MODEL], mod[:, D_MODEL:2 * D_MODEL]
    ms = jnp.mean(x * x, axis=-1, keepdims=True)
    h = x * lax.rsqrt(ms + EPS) * ng_ref[...] * (1.0 + scale) + shift
    hb_ref[...] = h.astype(BF16)


def _forget_gates(p_ref, lb_ref, k_ref, lhi_ref, llo_ref, tick):
    for hd in range(N_HEADS):
        ls = slice(hd * HG_DK, (hd + 1) * HG_DK)
        lbh = lb_ref[:, ls]
        f = lbh + (1.0 - lbh) * _sigmoid(p_ref[:, OFF_HF + ls.start:OFF_HF + ls.stop])
        k_ref[:, ls] = 1.0 - f
        logf = jnp.log(f)
        hi = logf.astype(BF16)
        lhi_ref[:, ls] = hi
        llo_ref[:, ls] = (logf - hi.astype(F32)).astype(BF16)
        tick()


def _gate_product(p_ref, g_off, z_off, gate_ref, out_off, tick):
    for hd in range(N_HEADS):
        ls = slice(hd * HEAD_DV, (hd + 1) * HEAD_DV)
        g = _sigmoid(p_ref[:, g_off + ls.start:g_off + ls.stop])
        z = _silu(p_ref[:, z_off + ls.start:z_off + ls.stop])
        gate_ref[:, out_off + ls.start:out_off + ls.stop] = (g * z).astype(BF16)
        tick()


def _padded_tril():
    row = lax.broadcasted_iota(jnp.int32, (HG_CHUNK, LANES), 0)
    col = lax.broadcasted_iota(jnp.int32, (HG_CHUNK, LANES), 1)
    return row >= col, col


def _hgrn_operands(p_ref, k_ref, b_ref, hg_ref, kht_ref, amat_ref, dec_ref, tick):
    T = p_ref.shape[0]
    tril = (lax.broadcasted_iota(jnp.int32, (HG_CHUNK, HG_CHUNK), 0)
            >= lax.broadcasted_iota(jnp.int32, (HG_CHUNK, HG_CHUNK), 1))
    amat_ref[...] = jnp.zeros_like(amat_ref)
    for c in range(T // HG_CHUNK):
        rs = slice(c * HG_CHUNK, (c + 1) * HG_CHUNK)
        for hd in range(N_HEADS):
            ls = slice(hd * HG_DK, (hd + 1) * HG_DK)
            q = _silu(p_ref[rs, OFF_HQ + ls.start:OFF_HQ + ls.stop])
            bb = b_ref[rs, ls]
            blast = bb[HG_CHUNK - 1:HG_CHUNK, :]
            qt = q * jnp.exp(bb)
            kh = (k_ref[rs, ls] * jnp.exp(blast - bb)).astype(BF16)
            hg_ref[rs, HG_QT + ls.start:HG_QT + ls.stop] = qt.astype(BF16)
            kht_ref[c, ls, :] = kh.T
            hg_ref[rs, HG_V + ls.start:HG_V + ls.stop] = p_ref[rs, OFF_HI + ls.start:OFF_HI + ls.stop].astype(BF16)
            dec_ref[c, :, ls] = jnp.exp(blast)
            qh = (qt * jnp.exp(jnp.minimum(-blast, EXP_CLAMP))).astype(BF16)
            a = lax.dot_general(qh, kh, NT_DIMS, preferred_element_type=F32)
            amat_ref[rs, ls.start:ls.start + HG_CHUNK] = jnp.where(tril, a, 0.0).astype(BF16)
            tick()


def _hgrn_scores_exact(p_ref, k_ref, b_ref, amat_ref):
    T = p_ref.shape[0]
    tril, col = _padded_tril()
    for hd in range(N_HEADS):
        ls = slice(hd * HG_DK, (hd + 1) * HG_DK)

        def chunk_body(c, carry):
            r0 = pl.multiple_of(c * HG_CHUNK, HG_CHUNK)
            rows = pl.ds(r0, HG_CHUNK)
            q = _silu(p_ref[rows, OFF_HQ + ls.start:OFF_HQ + ls.stop])
            bb = b_ref[rows, ls]

            def group_body(g, acc):
                base = pl.multiple_of(r0 + g * SUBLANES, SUBLANES)
                kg = k_ref[pl.ds(base, SUBLANES), ls]
                bg = b_ref[pl.ds(base, SUBLANES), ls]
                for j in range(SUBLANES):
                    e = jnp.exp(jnp.minimum(bb - bg[j:j + 1, :], 0.0))
                    colv = jnp.sum(q * kg[j:j + 1, :] * e, axis=-1, keepdims=True)
                    acc = jnp.where(col == g * SUBLANES + j, colv, acc)
                return acc

            a = lax.fori_loop(0, HG_CHUNK // SUBLANES, group_body, jnp.zeros((HG_CHUNK, LANES), F32))
            amat_ref[rows, ls] = jnp.where(tril, a, 0.0).astype(BF16)
            return carry

        lax.fori_loop(0, T // HG_CHUNK, chunk_body, 0)


def _retention_operands(p_ref, cos_ref, sina_ref, sinb_ref, xi_ref, zeta_ref, ret_ref):
    cos, sina, sinb = cos_ref[...], sina_ref[...], sinb_ref[...]

    def rot(u):
        return u * cos + pltpu.roll(u, 2 * RET_DK - RET_DK // 2, 1) * sina + pltpu.roll(u, RET_DK // 2, 1) * sinb

    for pr in range(N_PAIRS):
        qk = slice(pr * 2 * RET_DK, (pr + 1) * 2 * RET_DK)
        qr = rot(p_ref[:, OFF_RQ + qk.start:OFF_RQ + qk.stop])
        kr = rot(p_ref[:, OFF_RK + qk.start:OFF_RK + qk.stop]) * (RET_DK ** -0.5)
        ret_ref[:, RT_Q + qk.start:RT_Q + qk.stop] = qr.astype(BF16)
        ret_ref[:, RT_K + qk.start:RT_K + qk.stop] = kr.astype(BF16)
        ret_ref[:, RT_KZ + qk.start:RT_KZ + qk.stop] = (kr * zeta_ref[pr]).astype(BF16)
        ret_ref[:, RT_QXI + qk.start:RT_QXI + qk.stop] = (qr * xi_ref[pr]).astype(BF16)
    ret_ref[:, RT_V:RT_V + N_HEADS * HEAD_DV] = p_ref[:, OFF_RV:OFF_RV + N_HEADS * HEAD_DV].astype(BF16)


def _front_kernel(x_ref, mod_ref, ng_ref, win_ref, lb_ref, ltri_ref, cos_ref, sina_ref, sinb_ref,
                  xi_ref, zeta_ref,
                  hg_ref, kht_ref, ret_ref, gate_ref, amat_ref, dec_ref,
                  p_ref, hb_ref, k_ref, b_ref, lhi_ref, llo_ref):
    _normalize(x_ref[...], mod_ref[...], ng_ref, hb_ref)
    proj = _ProjectionStream(
        hb_ref, win_ref, p_ref,
        _pieces(OFF_HF, 1024) + _pieces(OFF_HZ, 1024) + _pieces(OFF_GA, 1024)
        + _pieces(OFF_RZ, 1024) + _pieces(OFF_GB, 1024) + _pieces(OFF_HQ, 1024) + _pieces(OFF_HI, 1024)
        + _pieces(OFF_RQ, 1024) + _pieces(OFF_RV, 1024))
    proj.emit(4)
    _forget_gates(p_ref, lb_ref, k_ref, lhi_ref, llo_ref, proj.emit)
    ltri = ltri_ref[...]
    b_ref[...] = _dot(ltri, lhi_ref[...]) + _dot(ltri, llo_ref[...])
    _gate_product(p_ref, OFF_GA, OFF_HZ, gate_ref, GT_A, proj.emit)
    _gate_product(p_ref, OFF_GB, OFF_RZ, gate_ref, GT_B, proj.emit)
    _hgrn_operands(p_ref, k_ref, b_ref, hg_ref, kht_ref, amat_ref, dec_ref, _every(4, proj.emit))
    _retention_operands(p_ref, cos_ref, sina_ref, sinb_ref, xi_ref, zeta_ref, ret_ref)

    worst = jnp.float32(0.0)
    for c in range(x_ref.shape[0] // HG_CHUNK):
        worst = jnp.maximum(worst, jnp.max(-b_ref[(c + 1) * HG_CHUNK - 1:(c + 1) * HG_CHUNK, :]))

    @pl.when(worst > EXP_CLAMP)
    def _():
        _hgrn_scores_exact(p_ref, k_ref, b_ref, amat_ref)


def _hgrn_recurrence(hg_ref, kht_ref, amat_ref, dec_ref, s_ref, oa_ref):
    T = hg_ref.shape[0]
    states = [s_ref[hd] for hd in range(N_HEADS)]
    for c in range(T // HG_CHUNK):
        rs = slice(c * HG_CHUNK, (c + 1) * HG_CHUNK)
        for hd in range(N_HEADS):
            ls = slice(hd * HG_DK, (hd + 1) * HG_DK)
            qt = hg_ref[rs, HG_QT + ls.start:HG_QT + ls.stop]
            v = hg_ref[rs, HG_V + ls.start:HG_V + ls.stop]
            st = states[hd]
            intra = _dot(amat_ref[rs, ls.start:ls.start + HG_CHUNK], v)
            oa_ref[rs, ls] = intra + _dot(qt, st.astype(BF16))
            decay = jnp.transpose(jnp.broadcast_to(dec_ref[c, :, ls], (HEAD_DV, HG_DK)))
            states[hd] = st * decay + _dot(kht_ref[c, ls, :], v)
    for hd in range(N_HEADS):
        s_ref[hd] = states[hd]


def _retention_recurrence(ret_ref, dm_ref, rdec_ref, r_ref, ob_ref):
    T = ret_ref.shape[0]
    first_head = lax.broadcasted_iota(jnp.int32, (RET_CHUNK, 2 * RET_DK), 1) < RET_DK
    bd_row = lax.broadcasted_iota(jnp.int32, (2 * RET_DK, 2 * HEAD_DV), 0)
    bd_col = lax.broadcasted_iota(jnp.int32, (2 * RET_DK, 2 * HEAD_DV), 1)
    bd_mask = (bd_row < RET_DK) == (bd_col < HEAD_DV)
    states = [r_ref[pr] for pr in range(N_PAIRS)]
    for c in range(T // RET_CHUNK):
        rs = slice(c * RET_CHUNK, (c + 1) * RET_CHUNK)
        for pr in range(N_PAIRS):
            qk = slice(pr * 2 * RET_DK, (pr + 1) * 2 * RET_DK)
            vs = slice(pr * 2 * HEAD_DV, (pr + 1) * 2 * HEAD_DV)
            rbd = states[pr]
            qr = ret_ref[rs, RT_Q + qk.start:RT_Q + qk.stop]
            kr = ret_ref[rs, RT_K + qk.start:RT_K + qk.stop]
            vpair = ret_ref[rs, RT_V + vs.start:RT_V + vs.stop]
            zero = jnp.zeros_like(qr)
            q0 = jnp.where(first_head, qr, zero)
            q1 = jnp.where(first_head, zero, qr)
            sc = lax.dot_general(jnp.concatenate([q0, q1], axis=0), kr, NT_DIMS, preferred_element_type=F32)
            sc0 = sc[:RET_CHUNK] * dm_ref[2 * pr]
            sc1 = sc[RET_CHUNK:] * dm_ref[2 * pr + 1]
            intra0 = _dot(sc0.astype(BF16), vpair[:, :HEAD_DV])
            intra1 = _dot(sc1.astype(BF16), vpair[:, HEAD_DV:])
            cross = _dot(ret_ref[rs, RT_QXI + qk.start:RT_QXI + qk.stop], rbd.astype(BF16))
            ob_ref[rs, vs] = jnp.concatenate([intra0, intra1], axis=1) + cross
            cstate = lax.dot_general(ret_ref[rs, RT_KZ + qk.start:RT_KZ + qk.stop], vpair, TN_DIMS,
                                     preferred_element_type=F32)
            states[pr] = rbd * rdec_ref[pr] + jnp.where(bd_mask, cstate, 0.0)
    for pr in range(N_PAIRS):
        r_ref[pr] = states[pr]


def _merge_out(x, gate, oa_ref, ob_ref, gate_ref, hgg_ref, rtg_ref, fg_ref, wout_ref, m_ref, o_ref):
    def normed(o, g):
        return o * lax.rsqrt(jnp.mean(o * o, axis=-1, keepdims=True) + EPS) * g

    for hd in range(N_HEADS):
        ls = slice(hd * HEAD_DV, (hd + 1) * HEAD_DV)
        ua = normed(oa_ref[:, ls], hgg_ref[:, ls]) * gate_ref[:, GT_A + ls.start:GT_A + ls.stop]
        ub = normed(ob_ref[:, ls], rtg_ref[:, ls]) * gate_ref[:, GT_B + ls.start:GT_B + ls.stop]
        m_ref[:, ls] = (ua + ub).astype(BF16)
    xo = x + gate * _dot(m_ref[...], wout_ref[...])
    o_ref[...] = xo * lax.rsqrt(jnp.mean(xo * xo, axis=-1, keepdims=True) + EPS) * fg_ref[...]


def _back_kernel(x_ref, mod_ref, hg_ref, kht_ref, ret_ref, gate_ref, amat_ref, dec_ref, hgg_ref, rtg_ref, fg_ref,
                 wout_ref, dm_ref, rdec_ref,
                 o_ref,
                 s_ref, r_ref, oa_ref, ob_ref, m_ref):
    @pl.when(pl.program_id(1) == 0)
    def _():
        s_ref[...] = jnp.zeros_like(s_ref)
        r_ref[...] = jnp.zeros_like(r_ref)

    _hgrn_recurrence(hg_ref, kht_ref, amat_ref, dec_ref, s_ref, oa_ref)
    _retention_recurrence(ret_ref, dm_ref, rdec_ref, r_ref, ob_ref)
    _merge_out(x_ref[...], mod_ref[:, 2 * D_MODEL:], oa_ref, ob_ref, gate_ref, hgg_ref, rtg_ref, fg_ref,
               wout_ref, m_ref, o_ref)


def _rotary_tables(seq_len):
    half = RET_DK // 2
    inv_freq = 1.0 / (ROPE_BASE ** jnp.linspace(0.0, 1.0, half, dtype=F32))
    ang = jnp.arange(seq_len, dtype=jnp.int32).astype(F32)[:, None] * inv_freq[None, :]
    cos, sin = jnp.cos(ang), jnp.sin(ang)
    zero = jnp.zeros_like(sin)
    cos_t = jnp.tile(cos, (1, 4))
    sina_t = jnp.tile(jnp.concatenate([-sin, zero], axis=1), (1, 2))
    sinb_t = jnp.tile(jnp.concatenate([zero, sin], axis=1), (1, 2))
    return cos_t, sina_t, sinb_t


def _retention_tables(tile_t):
    c = RET_CHUNK
    log_gamma = jnp.log(1.0 - jnp.exp2(-5.0 - jnp.arange(N_HEADS, dtype=F32)))
    idx = jnp.arange(c, dtype=F32)
    rel = idx[:, None] - idx[None, :]
    dm = jnp.where(rel >= 0, jnp.exp(log_gamma[:, None, None] * jnp.maximum(rel, 0.0)), 0.0)
    zeta = jnp.exp(log_gamma[:, None] * (c - 1.0 - idx))
    xi = jnp.exp(log_gamma[:, None] * (idx + 1.0))
    decay = jnp.exp(log_gamma * c)

    def per_pair_lanes(t):
        t = jnp.transpose(t.reshape(N_PAIRS, 2, c), (0, 2, 1))
        return jnp.tile(jnp.repeat(t, RET_DK, axis=2), (1, tile_t // c, 1))

    dec = jnp.repeat(decay.reshape(N_PAIRS, 1, 2), HEAD_DV, axis=2)
    return dm, per_pair_lanes(xi), per_pair_lanes(zeta), dec


def _chunk_tril(tile_t):
    r = jnp.arange(tile_t, dtype=jnp.int32)
    same_chunk = (r[:, None] // HG_CHUNK) == (r[None, :] // HG_CHUNK)
    return jnp.where(same_chunk & (r[:, None] >= r[None, :]), 1.0, 0.0).astype(BF16)


def _const_spec(shape, single_buffer=False):
    zeros = (0,) * len(shape)
    if single_buffer:
        return pl.BlockSpec(shape, lambda b, t: zeros, pipeline_mode=pl.Buffered(1))
    return pl.BlockSpec(shape, lambda b, t: zeros)


def _prep_call(c, w_ada, b_ada, lb_logits):
    bsz = c.shape[0]
    return pl.pallas_call(
        _prep_kernel,
        out_shape=(jax.ShapeDtypeStruct((bsz, 3 * D_MODEL), F32),
                   jax.ShapeDtypeStruct((1, N_HEADS * HG_DK), F32)),
        compiler_params=pltpu.CompilerParams(vmem_limit_bytes=V7X_VMEM_LIMIT_BYTES),
        name="adaln_prep",
    )(c, w_ada, b_ada.reshape(1, -1), lb_logits)


def _tile_spec(tile_t, width):
    return pl.BlockSpec((None, tile_t, width), lambda b, t: (b, t, 0))


def _front_call(x, mod3, norm_g, w_in, lb, tile_t):
    bsz, seq, d = x.shape
    n_hc = tile_t // HG_CHUNK
    cos_t, sina_t, sinb_t = _rotary_tables(seq)
    _, xi, zeta, _ = _retention_tables(tile_t)
    ltri = _chunk_tril(tile_t)
    rot_spec = pl.BlockSpec((tile_t, 2 * RET_DK), lambda b, t: (t, 0))
    dec_spec = pl.BlockSpec((None, n_hc, 1, d), lambda b, t: (b, t, 0, 0))
    kht_spec = pl.BlockSpec((None, n_hc, d, HG_CHUNK), lambda b, t: (b, t, 0, 0))
    bf = lambda width: jax.ShapeDtypeStruct((bsz, seq, width), BF16)
    return pl.pallas_call(
        _front_kernel,
        grid=(bsz, seq // tile_t),
        in_specs=[
            _tile_spec(tile_t, d),
            pl.BlockSpec((None, 1, 3 * d), lambda b, t: (b, 0, 0)),
            _const_spec((1, d)),
            _const_spec((d, D_IN), single_buffer=True),
            _const_spec((1, d)),
            _const_spec(ltri.shape),
            rot_spec, rot_spec, rot_spec,
            _const_spec(xi.shape), _const_spec(zeta.shape),
        ],
        out_specs=[
            _tile_spec(tile_t, HG_WIDTH), kht_spec, _tile_spec(tile_t, RT_WIDTH), _tile_spec(tile_t, GT_WIDTH),
            _tile_spec(tile_t, d), dec_spec,
        ],
        out_shape=[bf(HG_WIDTH), jax.ShapeDtypeStruct((bsz, seq // HG_CHUNK, d, HG_CHUNK), BF16),
                   bf(RT_WIDTH), bf(GT_WIDTH), bf(d),
                   jax.ShapeDtypeStruct((bsz, seq // HG_CHUNK, 1, d), F32)],
        scratch_shapes=[
            pltpu.VMEM((tile_t, D_IN), F32),
            pltpu.VMEM((tile_t, d), BF16),
            pltpu.VMEM((tile_t, d), F32),
            pltpu.VMEM((tile_t, d), F32),
            pltpu.VMEM((tile_t, d), BF16),
            pltpu.VMEM((tile_t, d), BF16),
        ],
        compiler_params=pltpu.CompilerParams(
            dimension_semantics=("arbitrary", "arbitrary"),
            vmem_limit_bytes=V7X_VMEM_LIMIT_BYTES),
        name="layer_front",
    )(x, mod3, norm_g.reshape(1, -1), w_in, lb, ltri, cos_t, sina_t, sinb_t, xi, zeta)


def _back_call(x, mod3, hg, kht, ret, gate, amat, dec, hg_g, ret_g, final_g, w_out, tile_t):
    bsz, seq, d = x.shape
    n_hc = tile_t // HG_CHUNK
    dm, _, _, rdec = _retention_tables(tile_t)
    row = lambda v: v.reshape(1, -1)
    return pl.pallas_call(
        _back_kernel,
        grid=(bsz, seq // tile_t),
        in_specs=[
            _tile_spec(tile_t, d),
            pl.BlockSpec((None, 1, 3 * d), lambda b, t: (b, 0, 0)),
            _tile_spec(tile_t, HG_WIDTH),
            pl.BlockSpec((None, n_hc, d, HG_CHUNK), lambda b, t: (b, t, 0, 0)),
            _tile_spec(tile_t, RT_WIDTH), _tile_spec(tile_t, GT_WIDTH),
            _tile_spec(tile_t, d),
            pl.BlockSpec((None, n_hc, 1, d), lambda b, t: (b, t, 0, 0)),
            _const_spec((1, d)), _const_spec((1, d)), _const_spec((1, d)),
            _const_spec((d, d), single_buffer=True),
            _const_spec(dm.shape), _const_spec(rdec.shape),
        ],
        out_specs=_tile_spec(tile_t, d),
        out_shape=jax.ShapeDtypeStruct(x.shape, x.dtype),
        scratch_shapes=[
            pltpu.VMEM((N_HEADS, HG_DK, HEAD_DV), F32),
            pltpu.VMEM((N_PAIRS, 2 * RET_DK, 2 * HEAD_DV), F32),
            pltpu.VMEM((tile_t, d), F32),
            pltpu.VMEM((tile_t, d), F32),
            pltpu.VMEM((tile_t, d), BF16),
        ],
        compiler_params=pltpu.CompilerParams(
            dimension_semantics=("arbitrary", "arbitrary"),
            vmem_limit_bytes=V7X_VMEM_LIMIT_BYTES),
        name="layer_back",
    )(x, mod3, hg, kht, ret, gate, amat, dec, row(hg_g), row(ret_g), row(final_g), w_out, dm, rdec)


def kernel(x, c, norm_g, w_ada, b_ada, w_in, hg_lb_logits, hg_norm_g, ret_norm_g, w_out, final_g):
    depth = norm_g.shape[0]
    assert depth == 1 and hg_lb_logits.shape[0] == 2
    bsz, seq, d = x.shape
    tile_t = min(TILE_T, seq)
    assert d == D_MODEL and seq % tile_t == 0 and tile_t % RET_CHUNK == 0
    mod, lb = _prep_call(c, w_ada[0], b_ada[0], hg_lb_logits)
    mod3 = mod.reshape(bsz, 1, 3 * d)
    hg, kht, ret, gate, amat, dec = _front_call(x, mod3, norm_g[0], w_in[0].astype(BF16), lb, tile_t)
    return _back_call(x, mod3, hg, kht, ret, gate, amat, dec, hg_norm_g[0], ret_norm_g[0], final_g,
                      w_out[0].astype(BF16), min(TILE_BACK, seq))
```

```python
import jax
import jax.numpy as jnp
from jax import lax
from jax.experimental import pallas as pl
from jax.experimental.pallas import tpu as pltpu

F32 = jnp.float32
BF16 = jnp.bfloat16

D_MODEL = 1024
N_HEADS = 8
HEAD_DV = 128
HG_DK = 128
RET_DK = 64
ROPE_BASE = 10000.0
EPS = 1e-6

TILE_T = 256
HG_CHUNK = 64
RET_CHUNK = 128
N_PAIRS = N_HEADS // 2
SUBLANES = 8
LANES = 128

OFF_HQ, OFF_HF, OFF_HI, OFF_HZ = 0, 1024, 2048, 3072
OFF_RQ, OFF_RK, OFF_RV, OFF_RZ = 4096, 4608, 5120, 6144
OFF_GA, OFF_GB = 7168, 8192
D_IN = 9216
PROJ_BLOCK = 256

HG_QT, HG_V, HG_WIDTH = 0, 1024, 2048
RT_Q, RT_K, RT_KZ, RT_QXI, RT_V, RT_WIDTH = 0, 512, 1024, 1536, 2048, 3072
GT_A, GT_B, GT_WIDTH = 0, 1024, 2048

EXP_CLAMP = 80.0

V7X_VMEM_LIMIT_BYTES = 58 * 1024 * 1024

NT_DIMS = (((1,), (1,)), ((), ()))
TN_DIMS = (((0,), (0,)), ((), ()))


def _sigmoid(x):
    return 0.5 * (jnp.tanh(0.5 * x) + 1.0)


def _silu(x):
    return x * _sigmoid(x)


def _dot(a, b):
    return jnp.dot(a, b, preferred_element_type=F32)


def _prep_kernel(c_ref, wada_ref, bada_ref, lbl_ref, mod_ref, lb_ref):
    c = c_ref[...]
    sc = _silu(c).astype(BF16)
    mod_ref[...] = _dot(sc, wada_ref[...].astype(BF16)) + bada_ref[...]
    lg = lbl_ref[...]
    l0, l1 = lg[0:1, :], lg[1:2, :]
    mx = jnp.maximum(l0, l1)
    e0, e1 = jnp.exp(l0 - mx), jnp.exp(l1 - mx)
    lb_ref[...] = e0 / (e0 + e1)


class _ProjectionStream:
    def __init__(self, hb_ref, win_ref, p_ref, pieces):
        self.hb_ref, self.win_ref, self.p_ref = hb_ref, win_ref, p_ref
        self.pieces = list(pieces)

    def emit(self, n_pieces=1):
        for _ in range(min(n_pieces, len(self.pieces))):
            cs = slice(self.pieces[0] * PROJ_BLOCK, (self.pieces[0] + 1) * PROJ_BLOCK)
            self.pieces.pop(0)
            self.p_ref[:, cs] = _dot(self.hb_ref[...], self.win_ref[:, cs])


def _pieces(offset, width):
    return list(range(offset // PROJ_BLOCK, (offset + width) // PROJ_BLOCK))


def _every(n_calls, fn):
    count = [0]

    def tick():
        count[0] += 1
        if count[0] % n_calls == 0:
            fn()

    return tick


def _normalize(x, mod, ng_ref, hb_ref):
    shift, scale = mod[:, :D_MODEL], mod[:, D_MODEL:2 * D_MODEL]
    ms = jnp.mean(x * x, axis=-1, keepdims=True)
    h = x * lax.rsqrt(ms + EPS) * ng_ref[...] * (1.0 + scale) + shift
    hb_ref[...] = h.astype(BF16)


def _forget_gates(p_ref, lb_ref, k_ref, lhi_ref, llo_ref, tick):
    for hd in range(N_HEADS):
        ls = slice(hd * HG_DK, (hd + 1) * HG_DK)
        lbh = lb_ref[:, ls]
        f = lbh + (1.0 - lbh) * _sigmoid(p_ref[:, OFF_HF + ls.start:OFF_HF + ls.stop])
        k_ref[:, ls] = 1.0 - f
        logf = jnp.log(f)
        hi = logf.astype(BF16)
        lhi_ref[:, ls] = hi
        llo_ref[:, ls] = (logf - hi.astype(F32)).astype(BF16)
        tick()


def _gate_product(p_ref, g_off, z_off, gate_ref, out_off, tick):
    for hd in range(N_HEADS):
        ls = slice(hd * HEAD_DV, (hd + 1) * HEAD_DV)
        g = _sigmoid(p_ref[:, g_off + ls.start:g_off + ls.stop])
        z = _silu(p_ref[:, z_off + ls.start:z_off + ls.stop])
        gate_ref[:, out_off + ls.start:out_off + ls.stop] = (g * z).astype(BF16)
        tick()


def _padded_tril():
    row = lax.broadcasted_iota(jnp.int32, (HG_CHUNK, LANES), 0)
    col = lax.broadcasted_iota(jnp.int32, (HG_CHUNK, LANES), 1)
    return row >= col, col


def _hgrn_operands(p_ref, k_ref, b_ref, hg_ref, kht_ref, amat_ref, dec_ref, tick):
    T = p_ref.shape[0]
    tril = (lax.broadcasted_iota(jnp.int32, (HG_CHUNK, HG_CHUNK), 0)
            >= lax.broadcasted_iota(jnp.int32, (HG_CHUNK, HG_CHUNK), 1))
    amat_ref[...] = jnp.zeros_like(amat_ref)
    for c in range(T // HG_CHUNK):
        rs = slice(c * HG_CHUNK, (c + 1) * HG_CHUNK)
        for hd in range(N_HEADS):
            ls = slice(hd * HG_DK, (hd + 1) * HG_DK)
            q = _silu(p_ref[rs, OFF_HQ + ls.start:OFF_HQ + ls.stop])
            bb = b_ref[rs, ls]
            blast = bb[HG_CHUNK - 1:HG_CHUNK, :]
            qt = q * jnp.exp(bb)
            kh = (k_ref[rs, ls] * jnp.exp(blast - bb)).astype(BF16)
            hg_ref[rs, HG_QT + ls.start:HG_QT + ls.stop] = qt.astype(BF16)
            kht_ref[c, ls, :] = kh.T
            hg_ref[rs, HG_V + ls.start:HG_V + ls.stop] = p_ref[rs, OFF_HI + ls.start:OFF_HI + ls.stop].astype(BF16)
            dec_ref[c, :, ls] = jnp.exp(blast)
            qh = (qt * jnp.exp(jnp.minimum(-blast, EXP_CLAMP))).astype(BF16)
            a = lax.dot_general(qh, kh, NT_DIMS, preferred_element_type=F32)
            amat_ref[rs, ls.start:ls.start + HG_CHUNK] = jnp.where(tril, a, 0.0).astype(BF16)
            tick()


def _hgrn_scores_exact(p_ref, k_ref, b_ref, amat_ref):
    T = p_ref.shape[0]
    tril, col = _padded_tril()
    for hd in range(N_HEADS):
        ls = slice(hd * HG_DK, (hd + 1) * HG_DK)

        def chunk_body(c, carry):
            r0 = pl.multiple_of(c * HG_CHUNK, HG_CHUNK)
            rows = pl.ds(r0, HG_CHUNK)
            q = _silu(p_ref[rows, OFF_HQ + ls.start:OFF_HQ + ls.stop])
            bb = b_ref[rows, ls]

            def group_body(g, acc):
                base = pl.multiple_of(r0 + g * SUBLANES, SUBLANES)
                kg = k_ref[pl.ds(base, SUBLANES), ls]
                bg = b_ref[pl.ds(base, SUBLANES), ls]
                for j in range(SUBLANES):
                    e = jnp.exp(jnp.minimum(bb - bg[j:j + 1, :], 0.0))
                    colv = jnp.sum(q * kg[j:j + 1, :] * e, axis=-1, keepdims=True)
                    acc = jnp.where(col == g * SUBLANES + j, colv, acc)
                return acc

            a = lax.fori_loop(0, HG_CHUNK // SUBLANES, group_body, jnp.zeros((HG_CHUNK, LANES), F32))
            amat_ref[rows, ls] = jnp.where(tril, a, 0.0).astype(BF16)
            return carry

        lax.fori_loop(0, T // HG_CHUNK, chunk_body, 0)


def _retention_operands(p_ref, cos_ref, sina_ref, sinb_ref, xi_ref, zeta_ref, ret_ref):
    cos, sina, sinb = cos_ref[...], sina_ref[...], sinb_ref[...]

    def rot(u):
        return u * cos + pltpu.roll(u, 2 * RET_DK - RET_DK // 2, 1) * sina + pltpu.roll(u, RET_DK // 2, 1) * sinb

    for pr in range(N_PAIRS):
        qk = slice(pr * 2 * RET_DK, (pr + 1) * 2 * RET_DK)
        qr = rot(p_ref[:, OFF_RQ + qk.start:OFF_RQ + qk.stop])
        kr = rot(p_ref[:, OFF_RK + qk.start:OFF_RK + qk.stop]) * (RET_DK ** -0.5)
        ret_ref[:, RT_Q + qk.start:RT_Q + qk.stop] = qr.astype(BF16)
        ret_ref[:, RT_K + qk.start:RT_K + qk.stop] = kr.astype(BF16)
        ret_ref[:, RT_KZ + qk.start:RT_KZ + qk.stop] = (kr * zeta_ref[pr]).astype(BF16)
        ret_ref[:, RT_QXI + qk.start:RT_QXI + qk.stop] = (qr * xi_ref[pr]).astype(BF16)
    ret_ref[:, RT_V:RT_V + N_HEADS * HEAD_DV] = p_ref[:, OFF_RV:OFF_RV + N_HEADS * HEAD_DV].astype(BF16)


def _front_stage(x_ref, mod_ref, ng_ref, win_ref, lb_ref, ltri_ref, cos_ref, sina_ref, sinb_ref,
                 xi_ref, zeta_ref,
                 hg_ref, kht_ref, ret_ref, gate_ref, amat_ref, dec_ref,
                 p_ref, hb_ref, k_ref, b_ref, lhi_ref, llo_ref):
    _normalize(x_ref[...], mod_ref[...], ng_ref, hb_ref)
    proj = _ProjectionStream(
        hb_ref, win_ref, p_ref,
        _pieces(OFF_HF, 1024) + _pieces(OFF_HZ, 1024) + _pieces(OFF_GA, 1024)
        + _pieces(OFF_RZ, 1024) + _pieces(OFF_GB, 1024) + _pieces(OFF_HQ, 1024) + _pieces(OFF_HI, 1024)
        + _pieces(OFF_RQ, 1024) + _pieces(OFF_RV, 1024))
    proj.emit(4)
    _forget_gates(p_ref, lb_ref, k_ref, lhi_ref, llo_ref, proj.emit)
    ltri = ltri_ref[...]
    b_ref[...] = _dot(ltri, lhi_ref[...]) + _dot(ltri, llo_ref[...])
    _gate_product(p_ref, OFF_GA, OFF_HZ, gate_ref, GT_A, proj.emit)
    _gate_product(p_ref, OFF_GB, OFF_RZ, gate_ref, GT_B, proj.emit)
    _hgrn_operands(p_ref, k_ref, b_ref, hg_ref, kht_ref, amat_ref, dec_ref, _every(4, proj.emit))
    _retention_operands(p_ref, cos_ref, sina_ref, sinb_ref, xi_ref, zeta_ref, ret_ref)

    worst = jnp.float32(0.0)
    for c in range(x_ref.shape[0] // HG_CHUNK):
        worst = jnp.maximum(worst, jnp.max(-b_ref[(c + 1) * HG_CHUNK - 1:(c + 1) * HG_CHUNK, :]))

    @pl.when(worst > EXP_CLAMP)
    def _():
        _hgrn_scores_exact(p_ref, k_ref, b_ref, amat_ref)


def _hgrn_recurrence(hg_ref, kht_ref, amat_ref, dec_ref, s_ref, oa_ref):
    T = hg_ref.shape[0]
    states = [s_ref[hd] for hd in range(N_HEADS)]
    for c in range(T // HG_CHUNK):
        rs = slice(c * HG_CHUNK, (c + 1) * HG_CHUNK)
        for hd in range(N_HEADS):
            ls = slice(hd * HG_DK, (hd + 1) * HG_DK)
            qt = hg_ref[rs, HG_QT + ls.start:HG_QT + ls.stop]
            v = hg_ref[rs, HG_V + ls.start:HG_V + ls.stop]
            st = states[hd]
            intra = _dot(amat_ref[rs, ls.start:ls.start + HG_CHUNK], v)
            oa_ref[rs, ls] = intra + _dot(qt, st.astype(BF16))
            decay = jnp.transpose(jnp.broadcast_to(dec_ref[c, :, ls], (HEAD_DV, HG_DK)))
            states[hd] = st * decay + _dot(kht_ref[c, ls, :], v)
    for hd in range(N_HEADS):
        s_ref[hd] = states[hd]


def _retention_recurrence(ret_ref, dm_ref, rdec_ref, r_ref, ob_ref):
    T = ret_ref.shape[0]
    first_head = lax.broadcasted_iota(jnp.int32, (RET_CHUNK, 2 * RET_DK), 1) < RET_DK
    bd_row = lax.broadcasted_iota(jnp.int32, (2 * RET_DK, 2 * HEAD_DV), 0)
    bd_col = lax.broadcasted_iota(jnp.int32, (2 * RET_DK, 2 * HEAD_DV), 1)
    bd_mask = (bd_row < RET_DK) == (bd_col < HEAD_DV)
    states = [r_ref[pr] for pr in range(N_PAIRS)]
    for c in range(T // RET_CHUNK):
        rs = slice(c * RET_CHUNK, (c + 1) * RET_CHUNK)
        for pr in range(N_PAIRS):
            qk = slice(pr * 2 * RET_DK, (pr + 1) * 2 * RET_DK)
            vs = slice(pr * 2 * HEAD_DV, (pr + 1) * 2 * HEAD_DV)
            rbd = states[pr]
            qr = ret_ref[rs, RT_Q + qk.start:RT_Q + qk.stop]
            kr = ret_ref[rs, RT_K + qk.start:RT_K + qk.stop]
            vpair = ret_ref[rs, RT_V + vs.start:RT_V + vs.stop]
            zero = jnp.zeros_like(qr)
            q0 = jnp.where(first_head, qr, zero)
            q1 = jnp.where(first_head, zero, qr)
            sc = lax.dot_general(jnp.concatenate([q0, q1], axis=0), kr, NT_DIMS, preferred_element_type=F32)
            sc0 = sc[:RET_CHUNK] * dm_ref[2 * pr]
            sc1 = sc[RET_CHUNK:] * dm_ref[2 * pr + 1]
            intra0 = _dot(sc0.astype(BF16), vpair[:, :HEAD_DV])
            intra1 = _dot(sc1.astype(BF16), vpair[:, HEAD_DV:])
            cross = _dot(ret_ref[rs, RT_QXI + qk.start:RT_QXI + qk.stop], rbd.astype(BF16))
            ob_ref[rs, vs] = jnp.concatenate([intra0, intra1], axis=1) + cross
            cstate = lax.dot_general(ret_ref[rs, RT_KZ + qk.start:RT_KZ + qk.stop], vpair, TN_DIMS,
                                     preferred_element_type=F32)
            states[pr] = rbd * rdec_ref[pr] + jnp.where(bd_mask, cstate, 0.0)
    for pr in range(N_PAIRS):
        r_ref[pr] = states[pr]


def _merge_out(x, gate, oa_ref, ob_ref, gate_ref, hgg_ref, rtg_ref, fg_ref, wout_ref, m_ref, o_ref):
    def normed(o, g):
        return o * lax.rsqrt(jnp.mean(o * o, axis=-1, keepdims=True) + EPS) * g

    for hd in range(N_HEADS):
        ls = slice(hd * HEAD_DV, (hd + 1) * HEAD_DV)
        ua = normed(oa_ref[:, ls], hgg_ref[:, ls]) * gate_ref[:, GT_A + ls.start:GT_A + ls.stop]
        ub = normed(ob_ref[:, ls], rtg_ref[:, ls]) * gate_ref[:, GT_B + ls.start:GT_B + ls.stop]
        m_ref[:, ls] = (ua + ub).astype(BF16)
    xo = x + gate * _dot(m_ref[...], wout_ref[...])
    o_ref[...] = xo * lax.rsqrt(jnp.mean(xo * xo, axis=-1, keepdims=True) + EPS) * fg_ref[...]


def _layer_kernel(x_ref, mod_ref, ng_ref, win_ref, lb_ref, ltri_ref, cos_ref, sina_ref, sinb_ref,
                  xi_ref, zeta_ref, hgg_ref, rtg_ref, fg_ref, wout_ref, dm_ref, rdec_ref,
                  o_ref,
                  p_ref, hb_ref, k_ref, b_ref, lhi_ref, llo_ref,
                  hg_ref, kht_ref, ret_ref, gate_ref, amat_ref, dec_ref,
                  s_ref, r_ref, oa_ref, ob_ref, m_ref):
    @pl.when(pl.program_id(1) == 0)
    def _():
        s_ref[...] = jnp.zeros_like(s_ref)
        r_ref[...] = jnp.zeros_like(r_ref)

    _front_stage(x_ref, mod_ref, ng_ref, win_ref, lb_ref, ltri_ref, cos_ref, sina_ref, sinb_ref,
                 xi_ref, zeta_ref,
                 hg_ref, kht_ref, ret_ref, gate_ref, amat_ref, dec_ref,
                 p_ref, hb_ref, k_ref, b_ref, lhi_ref, llo_ref)
    _hgrn_recurrence(hg_ref, kht_ref, amat_ref, dec_ref, s_ref, oa_ref)
    _retention_recurrence(ret_ref, dm_ref, rdec_ref, r_ref, ob_ref)
    _merge_out(x_ref[...], mod_ref[:, 2 * D_MODEL:], oa_ref, ob_ref, gate_ref, hgg_ref, rtg_ref, fg_ref,
               wout_ref, m_ref, o_ref)


def _rotary_tables(seq_len):
    half = RET_DK // 2
    inv_freq = 1.0 / (ROPE_BASE ** jnp.linspace(0.0, 1.0, half, dtype=F32))
    ang = jnp.arange(seq_len, dtype=jnp.int32).astype(F32)[:, None] * inv_freq[None, :]
    cos, sin = jnp.cos(ang), jnp.sin(ang)
    zero = jnp.zeros_like(sin)
    cos_t = jnp.tile(cos, (1, 4))
    sina_t = jnp.tile(jnp.concatenate([-sin, zero], axis=1), (1, 2))
    sinb_t = jnp.tile(jnp.concatenate([zero, sin], axis=1), (1, 2))
    return cos_t, sina_t, sinb_t


def _retention_tables(tile_t):
    c = RET_CHUNK
    log_gamma = jnp.log(1.0 - jnp.exp2(-5.0 - jnp.arange(N_HEADS, dtype=F32)))
    idx = jnp.arange(c, dtype=F32)
    rel = idx[:, None] - idx[None, :]
    dm = jnp.where(rel >= 0, jnp.exp(log_gamma[:, None, None] * jnp.maximum(rel, 0.0)), 0.0)
    zeta = jnp.exp(log_gamma[:, None] * (c - 1.0 - idx))
    xi = jnp.exp(log_gamma[:, None] * (idx + 1.0))
    decay = jnp.exp(log_gamma * c)

    def per_pair_lanes(t):
        t = jnp.transpose(t.reshape(N_PAIRS, 2, c), (0, 2, 1))
        return jnp.tile(jnp.repeat(t, RET_DK, axis=2), (1, tile_t // c, 1))

    dec = jnp.repeat(decay.reshape(N_PAIRS, 1, 2), HEAD_DV, axis=2)
    return dm, per_pair_lanes(xi), per_pair_lanes(zeta), dec


def _chunk_tril(tile_t):
    r = jnp.arange(tile_t, dtype=jnp.int32)
    same_chunk = (r[:, None] // HG_CHUNK) == (r[None, :] // HG_CHUNK)
    return jnp.where(same_chunk & (r[:, None] >= r[None, :]), 1.0, 0.0).astype(BF16)


def _const_spec(shape, single_buffer=False):
    zeros = (0,) * len(shape)
    if single_buffer:
        return pl.BlockSpec(shape, lambda b, t: zeros, pipeline_mode=pl.Buffered(1))
    return pl.BlockSpec(shape, lambda b, t: zeros)


def _prep_call(c, w_ada, b_ada, lb_logits):
    bsz = c.shape[0]
    return pl.pallas_call(
        _prep_kernel,
        out_shape=(jax.ShapeDtypeStruct((bsz, 3 * D_MODEL), F32),
                   jax.ShapeDtypeStruct((1, N_HEADS * HG_DK), F32)),
        compiler_params=pltpu.CompilerParams(vmem_limit_bytes=V7X_VMEM_LIMIT_BYTES),
        name="adaln_prep",
    )(c, w_ada, b_ada.reshape(1, -1), lb_logits)


def _tile_spec(tile_t, width):
    return pl.BlockSpec((None, tile_t, width), lambda b, t: (b, t, 0))


def _layer_call(x, mod3, norm_g, w_in, lb, hg_g, ret_g, final_g, w_out, tile_t):
    bsz, seq, d = x.shape
    n_hc = tile_t // HG_CHUNK
    cos_t, sina_t, sinb_t = _rotary_tables(seq)
    dm, xi, zeta, rdec = _retention_tables(tile_t)
    ltri = _chunk_tril(tile_t)
    rot_spec = pl.BlockSpec((tile_t, 2 * RET_DK), lambda b, t: (t, 0))
    row = lambda v: v.reshape(1, -1)
    return pl.pallas_call(
        _layer_kernel,
        grid=(bsz, seq // tile_t),
        in_specs=[
            _tile_spec(tile_t, d),
            pl.BlockSpec((None, 1, 3 * d), lambda b, t: (b, 0, 0)),
            _const_spec((1, d)),
            _const_spec((d, D_IN), single_buffer=True),
            _const_spec((1, d)),
            _const_spec(ltri.shape),
            rot_spec, rot_spec, rot_spec,
            _const_spec(xi.shape), _const_spec(zeta.shape),
            _const_spec((1, d)), _const_spec((1, d)), _const_spec((1, d)),
            _const_spec((d, d), single_buffer=True),
            _const_spec(dm.shape), _const_spec(rdec.shape),
        ],
        out_specs=_tile_spec(tile_t, d),
        out_shape=jax.ShapeDtypeStruct(x.shape, x.dtype),
        scratch_shapes=[
            pltpu.VMEM((tile_t, D_IN), F32),
            pltpu.VMEM((tile_t, d), BF16),
            pltpu.VMEM((tile_t, d), F32),
            pltpu.VMEM((tile_t, d), F32),
            pltpu.VMEM((tile_t, d), BF16),
            pltpu.VMEM((tile_t, d), BF16),
            pltpu.VMEM((tile_t, HG_WIDTH), BF16),
            pltpu.VMEM((n_hc, d, HG_CHUNK), BF16),
            pltpu.VMEM((tile_t, RT_WIDTH), BF16),
            pltpu.VMEM((tile_t, GT_WIDTH), BF16),
            pltpu.VMEM((tile_t, d), BF16),
            pltpu.VMEM((n_hc, 1, d), F32),
            pltpu.VMEM((N_HEADS, HG_DK, HEAD_DV), F32),
            pltpu.VMEM((N_PAIRS, 2 * RET_DK, 2 * HEAD_DV), F32),
            pltpu.VMEM((tile_t, d), F32),
            pltpu.VMEM((tile_t, d), F32),
            pltpu.VMEM((tile_t, d), BF16),
        ],
        compiler_params=pltpu.CompilerParams(
            dimension_semantics=("arbitrary", "arbitrary"),
            vmem_limit_bytes=V7X_VMEM_LIMIT_BYTES),
        name="hybrid_layer",
    )(x, mod3, norm_g.reshape(1, -1), w_in, lb, ltri, cos_t, sina_t, sinb_t, xi, zeta,
      row(hg_g), row(ret_g), row(final_g), w_out, dm, rdec)


def kernel(x, c, norm_g, w_ada, b_ada, w_in, hg_lb_logits, hg_norm_g, ret_norm_g, w_out, final_g):
    depth = norm_g.shape[0]
    assert depth == 1 and hg_lb_logits.shape[0] == 2
    bsz, seq, d = x.shape
    tile_t = min(TILE_T, seq)
    assert d == D_MODEL and seq % tile_t == 0 and tile_t % RET_CHUNK == 0
    mod, lb = _prep_call(c, w_ada[0], b_ada[0], hg_lb_logits)
    mod3 = mod.reshape(bsz, 1, 3 * d)
    return _layer_call(x, mod3, norm_g[0], w_in[0].astype(BF16), lb, hg_norm_g[0], ret_norm_g[0], final_g,
                       w_out[0].astype(BF16), tile_t)
```

```python
import jax
import jax.numpy as jnp
from jax import lax
from jax.experimental import pallas as pl
from jax.experimental.pallas import tpu as pltpu

F32 = jnp.float32
BF16 = jnp.bfloat16

D_MODEL = 1024
N_HEADS = 8
HEAD_DV = 128
HG_DK = 128
RET_DK = 64
ROPE_BASE = 10000.0
EPS = 1e-6

TILE_T = 256
TILE_BACK = 512
HG_CHUNK = 64
RET_CHUNK = 128
N_PAIRS = N_HEADS // 2
SUBLANES = 8
LANES = 128

OFF_HQ, OFF_HF, OFF_HI, OFF_HZ = 0, 1024, 2048, 3072
OFF_RQ, OFF_RK, OFF_RV, OFF_RZ = 4096, 4608, 5120, 6144
OFF_GA, OFF_GB = 7168, 8192
D_IN = 9216
PROJ_BLOCK = 256

HG_QT, HG_V, HG_WIDTH = 0, 1024, 2048
RT_Q, RT_K, RT_KZ, RT_QXI, RT_V, RT_WIDTH = 0, 512, 1024, 1536, 2048, 3072
GT_A, GT_B, GT_WIDTH = 0, 1024, 2048

EXP_CLAMP = 80.0

V7X_VMEM_LIMIT_BYTES = 58 * 1024 * 1024

NT_DIMS = (((1,), (1,)), ((), ()))
TN_DIMS = (((0,), (0,)), ((), ()))


def _sigmoid(x):
    return 0.5 * (jnp.tanh(0.5 * x) + 1.0)


def _silu(x):
    return x * _sigmoid(x)


def _dot(a, b):
    return jnp.dot(a, b, preferred_element_type=F32)


def _prep_kernel(c_ref, wada_ref, bada_ref, lbl_ref, mod_ref, lb_ref):
    c = c_ref[...]
    sc = _silu(c).astype(BF16)
    mod_ref[...] = _dot(sc, wada_ref[...].astype(BF16)) + bada_ref[...]
    lg = lbl_ref[...]
    l0, l1 = lg[0:1, :], lg[1:2, :]
    mx = jnp.maximum(l0, l1)
    e0, e1 = jnp.exp(l0 - mx), jnp.exp(l1 - mx)
    lb_ref[...] = e0 / (e0 + e1)


class _ProjectionStream:
    def __init__(self, hb_ref, win_ref, p_ref, pieces):
        self.hb_ref, self.win_ref, self.p_ref = hb_ref, win_ref, p_ref
        self.pieces = list(pieces)

    def emit(self, n_pieces=1):
        for _ in range(min(n_pieces, len(self.pieces))):
            cs = slice(self.pieces[0] * PROJ_BLOCK, (self.pieces[0] + 1) * PROJ_BLOCK)
            self.pieces.pop(0)
            self.p_ref[:, cs] = _dot(self.hb_ref[...], self.win_ref[:, cs])


def _pieces(offset, width):
    return list(range(offset // PROJ_BLOCK, (offset + width) // PROJ_BLOCK))


def _normalize(x, mod, ng_ref, hb_ref):
    shift, scale = mod[:, :D_MODEL], mod[:, D_MODEL:2 * D_MODEL]
    ms = jnp.mean(x * x, axis=-1, keepdims=True)
    h = x * lax.rsqrt(ms + EPS) * ng_ref[...] * (1.0 + scale) + shift
    hb_ref[...] = h.astype(BF16)


def _forget_gates(p_ref, lb_ref, k_ref, lhi_ref, llo_ref, tick):
    for hd in range(N_HEADS):
        ls = slice(hd * HG_DK, (hd + 1) * HG_DK)
        lbh = lb_ref[:, ls]
        f = lbh + (1.0 - lbh) * _sigmoid(p_ref[:, OFF_HF + ls.start:OFF_HF + ls.stop])
        k_ref[:, ls] = 1.0 - f
        logf = jnp.log(f)
        hi = logf.astype(BF16)
        lhi_ref[:, ls] = hi
        llo_ref[:, ls] = (logf - hi.astype(F32)).astype(BF16)
        tick()


def _gate_product(p_ref, g_off, z_off, gate_ref, out_off, tick):
    for hd in range(N_HEADS):
        ls = slice(hd * HEAD_DV, (hd + 1) * HEAD_DV)
        g = _sigmoid(p_ref[:, g_off + ls.start:g_off + ls.stop])
        z = _silu(p_ref[:, z_off + ls.start:z_off + ls.stop])
        gate_ref[:, out_off + ls.start:out_off + ls.stop] = (g * z).astype(BF16)
        tick()


def _padded_tril():
    row = lax.broadcasted_iota(jnp.int32, (HG_CHUNK, LANES), 0)
    col = lax.broadcasted_iota(jnp.int32, (HG_CHUNK, LANES), 1)
    return row >= col, col


def _hgrn_operands(p_ref, k_ref, b_ref, hg_ref, kht_ref, amat_ref, dec_ref):
    T = p_ref.shape[0]
    tril = (lax.broadcasted_iota(jnp.int32, (HG_CHUNK, HG_CHUNK), 0)
            >= lax.broadcasted_iota(jnp.int32, (HG_CHUNK, HG_CHUNK), 1))
    amat_ref[...] = jnp.zeros_like(amat_ref)
    for c in range(T // HG_CHUNK):
        rs = slice(c * HG_CHUNK, (c + 1) * HG_CHUNK)
        for hd in range(N_HEADS):
            ls = slice(hd * HG_DK, (hd + 1) * HG_DK)
            q = _silu(p_ref[rs, OFF_HQ + ls.start:OFF_HQ + ls.stop])
            bb = b_ref[rs, ls]
            blast = bb[HG_CHUNK - 1:HG_CHUNK, :]
            qt = q * jnp.exp(bb)
            kh = (k_ref[rs, ls] * jnp.exp(blast - bb)).astype(BF16)
            hg_ref[rs, HG_QT + ls.start:HG_QT + ls.stop] = qt.astype(BF16)
            kht_ref[c, ls, :] = kh.T
            hg_ref[rs, HG_V + ls.start:HG_V + ls.stop] = p_ref[rs, OFF_HI + ls.start:OFF_HI + ls.stop].astype(BF16)
            dec_ref[c, :, ls] = jnp.exp(blast)
            qh = (qt * jnp.exp(jnp.minimum(-blast, EXP_CLAMP))).astype(BF16)
            a = lax.dot_general(qh, kh, NT_DIMS, preferred_element_type=F32)
            amat_ref[rs, ls.start:ls.start + HG_CHUNK] = jnp.where(tril, a, 0.0).astype(BF16)


def _hgrn_scores_exact(p_ref, k_ref, b_ref, amat_ref):
    T = p_ref.shape[0]
    tril, col = _padded_tril()
    for hd in range(N_HEADS):
        ls = slice(hd * HG_DK, (hd + 1) * HG_DK)

        def chunk_body(c, carry):
            r0 = pl.multiple_of(c * HG_CHUNK, HG_CHUNK)
            rows = pl.ds(r0, HG_CHUNK)
            q = _silu(p_ref[rows, OFF_HQ + ls.start:OFF_HQ + ls.stop])
            bb = b_ref[rows, ls]

            def group_body(g, acc):
                base = pl.multiple_of(r0 + g * SUBLANES, SUBLANES)
                kg = k_ref[pl.ds(base, SUBLANES), ls]
                bg = b_ref[pl.ds(base, SUBLANES), ls]
                for j in range(SUBLANES):
                    e = jnp.exp(jnp.minimum(bb - bg[j:j + 1, :], 0.0))
                    colv = jnp.sum(q * kg[j:j + 1, :] * e, axis=-1, keepdims=True)
                    acc = jnp.where(col == g * SUBLANES + j, colv, acc)
                return acc

            a = lax.fori_loop(0, HG_CHUNK // SUBLANES, group_body, jnp.zeros((HG_CHUNK, LANES), F32))
            amat_ref[rows, ls] = jnp.where(tril, a, 0.0).astype(BF16)
            return carry

        lax.fori_loop(0, T // HG_CHUNK, chunk_body, 0)


def _retention_operands(p_ref, cos_ref, sina_ref, sinb_ref, xi_ref, zeta_ref, ret_ref):
    cos, sina, sinb = cos_ref[...], sina_ref[...], sinb_ref[...]

    def rot(u):
        return u * cos + pltpu.roll(u, 2 * RET_DK - RET_DK // 2, 1) * sina + pltpu.roll(u, RET_DK // 2, 1) * sinb

    for pr in range(N_PAIRS):
        qk = slice(pr * 2 * RET_DK, (pr + 1) * 2 * RET_DK)
        qr = rot(p_ref[:, OFF_RQ + qk.start:OFF_RQ + qk.stop])
        kr = rot(p_ref[:, OFF_RK + qk.start:OFF_RK + qk.stop]) * (RET_DK ** -0.5)
        ret_ref[:, RT_Q + qk.start:RT_Q + qk.stop] = qr.astype(BF16)
        ret_ref[:, RT_K + qk.start:RT_K + qk.stop] = kr.astype(BF16)
        ret_ref[:, RT_KZ + qk.start:RT_KZ + qk.stop] = (kr * zeta_ref[pr]).astype(BF16)
        ret_ref[:, RT_QXI + qk.start:RT_QXI + qk.stop] = (qr * xi_ref[pr]).astype(BF16)
    ret_ref[:, RT_V:RT_V + N_HEADS * HEAD_DV] = p_ref[:, OFF_RV:OFF_RV + N_HEADS * HEAD_DV].astype(BF16)


def _front_kernel(x_ref, mod_ref, ng_ref, win_ref, lb_ref, ltri_ref, cos_ref, sina_ref, sinb_ref,
                  xi_ref, zeta_ref,
                  hg_ref, kht_ref, ret_ref, gate_ref, amat_ref, dec_ref,
                  p_ref, hb_ref, k_ref, b_ref, lhi_ref, llo_ref):
    _normalize(x_ref[...], mod_ref[...], ng_ref, hb_ref)
    proj = _ProjectionStream(
        hb_ref, win_ref, p_ref,
        _pieces(OFF_HF, 1024) + _pieces(OFF_HZ, 1024) + _pieces(OFF_GA, 1024)
        + _pieces(OFF_RZ, 1024) + _pieces(OFF_GB, 1024) + _pieces(OFF_RQ, 1024) + _pieces(OFF_RV, 1024)
        + _pieces(OFF_HQ, 1024) + _pieces(OFF_HI, 1024))
    proj.emit(4)
    _forget_gates(p_ref, lb_ref, k_ref, lhi_ref, llo_ref, proj.emit)
    ltri = ltri_ref[...]
    b_ref[...] = _dot(ltri, lhi_ref[...]) + _dot(ltri, llo_ref[...])
    _gate_product(p_ref, OFF_GA, OFF_HZ, gate_ref, GT_A, proj.emit)
    _gate_product(p_ref, OFF_GB, OFF_RZ, gate_ref, GT_B, proj.emit)
    proj.emit(8)
    _retention_operands(p_ref, cos_ref, sina_ref, sinb_ref, xi_ref, zeta_ref, ret_ref)
    _hgrn_operands(p_ref, k_ref, b_ref, hg_ref, kht_ref, amat_ref, dec_ref)

    worst = jnp.float32(0.0)
    for c in range(x_ref.shape[0] // HG_CHUNK):
        worst = jnp.maximum(worst, jnp.max(-b_ref[(c + 1) * HG_CHUNK - 1:(c + 1) * HG_CHUNK, :]))

    @pl.when(worst > EXP_CLAMP)
    def _():
        _hgrn_scores_exact(p_ref, k_ref, b_ref, amat_ref)


def _hgrn_recurrence(hg_ref, kht_ref, amat_ref, dec_ref, s_ref, oa_ref):
    T = hg_ref.shape[0]
    states = [s_ref[hd] for hd in range(N_HEADS)]
    for c in range(T // HG_CHUNK):
        rs = slice(c * HG_CHUNK, (c + 1) * HG_CHUNK)
        for hd in range(N_HEADS):
            ls = slice(hd * HG_DK, (hd + 1) * HG_DK)
            qt = hg_ref[rs, HG_QT + ls.start:HG_QT + ls.stop]
            v = hg_ref[rs, HG_V + ls.start:HG_V + ls.stop]
            st = states[hd]
            intra = _dot(amat_ref[rs, ls.start:ls.start + HG_CHUNK], v)
            oa_ref[rs, ls] = intra + _dot(qt, st.astype(BF16))
            decay = jnp.transpose(jnp.broadcast_to(dec_ref[c, :, ls], (HEAD_DV, HG_DK)))
            states[hd] = st * decay + _dot(kht_ref[c, ls, :], v)
    for hd in range(N_HEADS):
        s_ref[hd] = states[hd]


def _retention_recurrence(ret_ref, dm_ref, rdec_ref, r_ref, ob_ref):
    T = ret_ref.shape[0]
    first_head = lax.broadcasted_iota(jnp.int32, (RET_CHUNK, 2 * RET_DK), 1) < RET_DK
    bd_row = lax.broadcasted_iota(jnp.int32, (2 * RET_DK, 2 * HEAD_DV), 0)
    bd_col = lax.broadcasted_iota(jnp.int32, (2 * RET_DK, 2 * HEAD_DV), 1)
    bd_mask = (bd_row < RET_DK) == (bd_col < HEAD_DV)
    states = [r_ref[pr] for pr in range(N_PAIRS)]
    for c in range(T // RET_CHUNK):
        rs = slice(c * RET_CHUNK, (c + 1) * RET_CHUNK)
        for pr in range(N_PAIRS):
            qk = slice(pr * 2 * RET_DK, (pr + 1) * 2 * RET_DK)
            vs = slice(pr * 2 * HEAD_DV, (pr + 1) * 2 * HEAD_DV)
            rbd = states[pr]
            qr = ret_ref[rs, RT_Q + qk.start:RT_Q + qk.stop]
            kr = ret_ref[rs, RT_K + qk.start:RT_K + qk.stop]
            vpair = ret_ref[rs, RT_V + vs.start:RT_V + vs.stop]
            zero = jnp.zeros_like(qr)
            q0 = jnp.where(first_head, qr, zero)
            q1 = jnp.where(first_head, zero, qr)
            sc = lax.dot_general(jnp.concatenate([q0, q1], axis=0), kr, NT_DIMS, preferred_element_type=F32)
            sc0 = sc[:RET_CHUNK] * dm_ref[2 * pr]
            sc1 = sc[RET_CHUNK:] * dm_ref[2 * pr + 1]
            intra0 = _dot(sc0.astype(BF16), vpair[:, :HEAD_DV])
            intra1 = _dot(sc1.astype(BF16), vpair[:, HEAD_DV:])
            cross = _dot(ret_ref[rs, RT_QXI + qk.start:RT_QXI + qk.stop], rbd.astype(BF16))
            ob_ref[rs, vs] = jnp.concatenate([intra0, intra1], axis=1) + cross
            cstate = lax.dot_general(ret_ref[rs, RT_KZ + qk.start:RT_KZ + qk.stop], vpair, TN_DIMS,
                                     preferred_element_type=F32)
            states[pr] = rbd * rdec_ref[pr] + jnp.where(bd_mask, cstate, 0.0)
    for pr in range(N_PAIRS):
        r_ref[pr] = states[pr]


def _merge_out(x, gate, oa_ref, ob_ref, gate_ref, hgg_ref, rtg_ref, fg_ref, wout_ref, m_ref, o_ref):
    def normed(o, g):
        return o * lax.rsqrt(jnp.mean(o * o, axis=-1, keepdims=True) + EPS) * g

    for hd in range(N_HEADS):
        ls = slice(hd * HEAD_DV, (hd + 1) * HEAD_DV)
        ua = normed(oa_ref[:, ls], hgg_ref[:, ls]) * gate_ref[:, GT_A + ls.start:GT_A + ls.stop]
        ub = normed(ob_ref[:, ls], rtg_ref[:, ls]) * gate_ref[:, GT_B + ls.start:GT_B + ls.stop]
        m_ref[:, ls] = (ua + ub).astype(BF16)
    xo = x + gate * _dot(m_ref[...], wout_ref[...])
    o_ref[...] = xo * lax.rsqrt(jnp.mean(xo * xo, axis=-1, keepdims=True) + EPS) * fg_ref[...]


def _back_kernel(x_ref, mod_ref, hg_ref, kht_ref, ret_ref, gate_ref, amat_ref, dec_ref, hgg_ref, rtg_ref, fg_ref,
                 wout_ref, dm_ref, rdec_ref,
                 o_ref,
                 s_ref, r_ref, oa_ref, ob_ref, m_ref):
    @pl.when(pl.program_id(1) == 0)
    def _():
        s_ref[...] = jnp.zeros_like(s_ref)
        r_ref[...] = jnp.zeros_like(r_ref)

    _hgrn_recurrence(hg_ref, kht_ref, amat_ref, dec_ref, s_ref, oa_ref)
    _retention_recurrence(ret_ref, dm_ref, rdec_ref, r_ref, ob_ref)
    _merge_out(x_ref[...], mod_ref[:, 2 * D_MODEL:], oa_ref, ob_ref, gate_ref, hgg_ref, rtg_ref, fg_ref,
               wout_ref, m_ref, o_ref)


def _rotary_tables(seq_len):
    half = RET_DK // 2
    inv_freq = 1.0 / (ROPE_BASE ** jnp.linspace(0.0, 1.0, half, dtype=F32))
    ang = jnp.arange(seq_len, dtype=jnp.int32).astype(F32)[:, None] * inv_freq[None, :]
    cos, sin = jnp.cos(ang), jnp.sin(ang)
    zero = jnp.zeros_like(sin)
    cos_t = jnp.tile(cos, (1, 4))
    sina_t = jnp.tile(jnp.concatenate([-sin, zero], axis=1), (1, 2))
    sinb_t = jnp.tile(jnp.concatenate([zero, sin], axis=1), (1, 2))
    return cos_t, sina_t, sinb_t


def _retention_tables(tile_t):
    c = RET_CHUNK
    log_gamma = jnp.log(1.0 - jnp.exp2(-5.0 - jnp.arange(N_HEADS, dtype=F32)))
    idx = jnp.arange(c, dtype=F32)
    rel = idx[:, None] - idx[None, :]
    dm = jnp.where(rel >= 0, jnp.exp(log_gamma[:, None, None] * jnp.maximum(rel, 0.0)), 0.0)
    zeta = jnp.exp(log_gamma[:, None] * (c - 1.0 - idx))
    xi = jnp.exp(log_gamma[:, None] * (idx + 1.0))
    decay = jnp.exp(log_gamma * c)

    def per_pair_lanes(t):
        t = jnp.transpose(t.reshape(N_PAIRS, 2, c), (0, 2, 1))
        return jnp.tile(jnp.repeat(t, RET_DK, axis=2), (1, tile_t // c, 1))

    dec = jnp.repeat(decay.reshape(N_PAIRS, 1, 2), HEAD_DV, axis=2)
    return dm, per_pair_lanes(xi), per_pair_lanes(zeta), dec


def _chunk_tril(tile_t):
    r = jnp.arange(tile_t, dtype=jnp.int32)
    same_chunk = (r[:, None] // HG_CHUNK) == (r[None, :] // HG_CHUNK)
    return jnp.where(same_chunk & (r[:, None] >= r[None, :]), 1.0, 0.0).astype(BF16)


def _const_spec(shape, single_buffer=False):
    zeros = (0,) * len(shape)
    if single_buffer:
        return pl.BlockSpec(shape, lambda b, t: zeros, pipeline_mode=pl.Buffered(1))
    return pl.BlockSpec(shape, lambda b, t: zeros)


def _prep_call(c, w_ada, b_ada, lb_logits):
    bsz = c.shape[0]
    return pl.pallas_call(
        _prep_kernel,
        out_shape=(jax.ShapeDtypeStruct((bsz, 3 * D_MODEL), F32),
                   jax.ShapeDtypeStruct((1, N_HEADS * HG_DK), F32)),
        compiler_params=pltpu.CompilerParams(vmem_limit_bytes=V7X_VMEM_LIMIT_BYTES),
        name="adaln_prep",
    )(c, w_ada, b_ada.reshape(1, -1), lb_logits)


def _tile_spec(tile_t, width):
    return pl.BlockSpec((None, tile_t, width), lambda b, t: (b, t, 0))


def _front_call(x, mod3, norm_g, w_in, lb, tile_t):
    bsz, seq, d = x.shape
    n_hc = tile_t // HG_CHUNK
    cos_t, sina_t, sinb_t = _rotary_tables(seq)
    _, xi, zeta, _ = _retention_tables(tile_t)
    ltri = _chunk_tril(tile_t)
    rot_spec = pl.BlockSpec((tile_t, 2 * RET_DK), lambda b, t: (t, 0))
    dec_spec = pl.BlockSpec((None, n_hc, 1, d), lambda b, t: (b, t, 0, 0))
    kht_spec = pl.BlockSpec((None, n_hc, d, HG_CHUNK), lambda b, t: (b, t, 0, 0))
    bf = lambda width: jax.ShapeDtypeStruct((bsz, seq, width), BF16)
    return pl.pallas_call(
        _front_kernel,
        grid=(bsz, seq // tile_t),
        in_specs=[
            _tile_spec(tile_t, d),
            pl.BlockSpec((None, 1, 3 * d), lambda b, t: (b, 0, 0)),
            _const_spec((1, d)),
            _const_spec((d, D_IN), single_buffer=True),
            _const_spec((1, d)),
            _const_spec(ltri.shape),
            rot_spec, rot_spec, rot_spec,
            _const_spec(xi.shape), _const_spec(zeta.shape),
        ],
        out_specs=[
            _tile_spec(tile_t, HG_WIDTH), kht_spec, _tile_spec(tile_t, RT_WIDTH), _tile_spec(tile_t, GT_WIDTH),
            _tile_spec(tile_t, d), dec_spec,
        ],
        out_shape=[bf(HG_WIDTH), jax.ShapeDtypeStruct((bsz, seq // HG_CHUNK, d, HG_CHUNK), BF16),
                   bf(RT_WIDTH), bf(GT_WIDTH), bf(d),
                   jax.ShapeDtypeStruct((bsz, seq // HG_CHUNK, 1, d), F32)],
        scratch_shapes=[
            pltpu.VMEM((tile_t, D_IN), F32),
            pltpu.VMEM((tile_t, d), BF16),
            pltpu.VMEM((tile_t, d), F32),
            pltpu.VMEM((tile_t, d), F32),
            pltpu.VMEM((tile_t, d), BF16),
            pltpu.VMEM((tile_t, d), BF16),
        ],
        compiler_params=pltpu.CompilerParams(
            dimension_semantics=("arbitrary", "arbitrary"),
            vmem_limit_bytes=V7X_VMEM_LIMIT_BYTES),
        name="layer_front",
    )(x, mod3, norm_g.reshape(1, -1), w_in, lb, ltri, cos_t, sina_t, sinb_t, xi, zeta)


def _back_call(x, mod3, hg, kht, ret, gate, amat, dec, hg_g, ret_g, final_g, w_out, tile_t):
    bsz, seq, d = x.shape
    n_hc = tile_t // HG_CHUNK
    dm, _, _, rdec = _retention_tables(tile_t)
    row = lambda v: v.reshape(1, -1)
    return pl.pallas_call(
        _back_kernel,
        grid=(bsz, seq // tile_t),
        in_specs=[
            _tile_spec(tile_t, d),
            pl.BlockSpec((None, 1, 3 * d), lambda b, t: (b, 0, 0)),
            _tile_spec(tile_t, HG_WIDTH),
            pl.BlockSpec((None, n_hc, d, HG_CHUNK), lambda b, t: (b, t, 0, 0)),
            _tile_spec(tile_t, RT_WIDTH), _tile_spec(tile_t, GT_WIDTH),
            _tile_spec(tile_t, d),
            pl.BlockSpec((None, n_hc, 1, d), lambda b, t: (b, t, 0, 0)),
            _const_spec((1, d)), _const_spec((1, d)), _const_spec((1, d)),
            _const_spec((d, d), single_buffer=True),
            _const_spec(dm.shape), _const_spec(rdec.shape),
        ],
        out_specs=_tile_spec(tile_t, d),
        out_shape=jax.ShapeDtypeStruct(x.shape, x.dtype),
        scratch_shapes=[
            pltpu.VMEM((N_HEADS, HG_DK, HEAD_DV), F32),
            pltpu.VMEM((N_PAIRS, 2 * RET_DK, 2 * HEAD_DV), F32),
            pltpu.VMEM((tile_t, d), F32),
            pltpu.VMEM((tile_t, d), F32),
            pltpu.VMEM((tile_t, d), BF16),
        ],
        compiler_params=pltpu.CompilerParams(
            dimension_semantics=("arbitrary", "arbitrary"),
            vmem_limit_bytes=V7X_VMEM_LIMIT_BYTES),
        name="layer_back",
    )(x, mod3, hg, kht, ret, gate, amat, dec, row(hg_g), row(ret_g), row(final_g), w_out, dm, rdec)


def kernel(x, c, norm_g, w_ada, b_ada, w_in, hg_lb_logits, hg_norm_g, ret_norm_g, w_out, final_g):
    depth = norm_g.shape[0]
    assert depth == 1 and hg_lb_logits.shape[0] == 2
    bsz, seq, d = x.shape
    tile_t = min(TILE_T, seq)
    assert d == D_MODEL and seq % tile_t == 0 and tile_t % RET_CHUNK == 0
    mod, lb = _prep_call(c, w_ada[0], b_ada[0], hg_lb_logits)
    mod3 = mod.reshape(bsz, 1, 3 * d)
    hg, kht, ret, gate, amat, dec = _front_call(x, mod3, norm_g[0], w_in[0].astype(BF16), lb, tile_t)
    return _back_call(x, mod3, hg, kht, ret, gate, amat, dec, hg_norm_g[0], ret_norm_g[0], final_g,
                      w_out[0].astype(BF16), min(TILE_BACK, seq))
```

```python
import jax
import jax.numpy as jnp
from jax import lax
from jax.experimental import pallas as pl
from jax.experimental.pallas import tpu as pltpu

F32 = jnp.float32
BF16 = jnp.bfloat16

D_MODEL = 1024
N_HEADS = 8
HEAD_DV = 128
HG_DK = 128
RET_DK = 64
ROPE_BASE = 10000.0
EPS = 1e-6

TILE_T = 256
TILE_BACK = 512
HG_CHUNK = 64
RET_CHUNK = 128
N_PAIRS = N_HEADS // 2
SUBLANES = 8
LANES = 128

OFF_HQ, OFF_HF, OFF_HI, OFF_HZ = 0, 1024, 2048, 3072
OFF_RQ, OFF_RK, OFF_RV, OFF_RZ = 4096, 4608, 5120, 6144
OFF_GA, OFF_GB = 7168, 8192
D_IN = 9216
PROJ_BLOCK = 256

HG_QT, HG_V, HG_WIDTH = 0, 1024, 2048
RT_Q, RT_K, RT_KZ, RT_QXI, RT_V, RT_WIDTH = 0, 512, 1024, 1536, 2048, 3072
GT_A, GT_B, GT_WIDTH = 0, 1024, 2048

EXP_CLAMP = 80.0

V7X_VMEM_LIMIT_BYTES = 58 * 1024 * 1024

NT_DIMS = (((1,), (1,)), ((), ()))
TN_DIMS = (((0,), (0,)), ((), ()))


def _sigmoid(x):
    return 0.5 * (jnp.tanh(0.5 * x) + 1.0)


def _silu(x):
    return x * _sigmoid(x)


def _dot(a, b):
    return jnp.dot(a, b, preferred_element_type=F32)


def _prep_kernel(c_ref, wada_ref, bada_ref, lbl_ref, mod_ref, lb_ref):
    c = c_ref[...]
    sc = _silu(c).astype(BF16)
    mod_ref[...] = _dot(sc, wada_ref[...].astype(BF16)) + bada_ref[...]
    lg = lbl_ref[...]
    l0, l1 = lg[0:1, :], lg[1:2, :]
    mx = jnp.maximum(l0, l1)
    e0, e1 = jnp.exp(l0 - mx), jnp.exp(l1 - mx)
    lb_ref[...] = e0 / (e0 + e1)


class _ProjectionStream:
    def __init__(self, hb_ref, win_ref, p_ref, pieces):
        self.hb_ref, self.win_ref, self.p_ref = hb_ref, win_ref, p_ref
        self.pieces = list(pieces)

    def emit(self, n_pieces=1):
        for _ in range(min(n_pieces, len(self.pieces))):
            cs = slice(self.pieces[0] * PROJ_BLOCK, (self.pieces[0] + 1) * PROJ_BLOCK)
            self.pieces.pop(0)
            self.p_ref[:, cs] = _dot(self.hb_ref[...], self.win_ref[:, cs])


def _pieces(offset, width):
    return list(range(offset // PROJ_BLOCK, (offset + width) // PROJ_BLOCK))


def _normalize(x, mod, ng_ref, hb_ref):
    shift, scale = mod[:, :D_MODEL], mod[:, D_MODEL:2 * D_MODEL]
    ms = jnp.mean(x * x, axis=-1, keepdims=True)
    h = x * lax.rsqrt(ms + EPS) * ng_ref[...] * (1.0 + scale) + shift
    hb_ref[...] = h.astype(BF16)


def _forget_gates(p_ref, lb_ref, k_ref, lhi_ref, llo_ref, tick):
    for hd in range(N_HEADS):
        ls = slice(hd * HG_DK, (hd + 1) * HG_DK)
        lbh = lb_ref[:, ls]
        f = lbh + (1.0 - lbh) * _sigmoid(p_ref[:, OFF_HF + ls.start:OFF_HF + ls.stop])
        k_ref[:, ls] = 1.0 - f
        logf = jnp.log(f)
        hi = logf.astype(BF16)
        lhi_ref[:, ls] = hi
        llo_ref[:, ls] = (logf - hi.astype(F32)).astype(BF16)
        tick()


def _gate_product(p_ref, g_off, z_off, gate_ref, out_off, tick):
    for hd in range(N_HEADS):
        ls = slice(hd * HEAD_DV, (hd + 1) * HEAD_DV)
        g = _sigmoid(p_ref[:, g_off + ls.start:g_off + ls.stop])
        z = _silu(p_ref[:, z_off + ls.start:z_off + ls.stop])
        gate_ref[:, out_off + ls.start:out_off + ls.stop] = (g * z).astype(BF16)
        tick()


def _padded_tril():
    row = lax.broadcasted_iota(jnp.int32, (HG_CHUNK, LANES), 0)
    col = lax.broadcasted_iota(jnp.int32, (HG_CHUNK, LANES), 1)
    return row >= col, col


def _hgrn_operands(p_ref, k_ref, b_ref, hg_ref, kht_ref, amat_ref, dec_ref):
    T = p_ref.shape[0]
    tril = (lax.broadcasted_iota(jnp.int32, (HG_CHUNK, HG_CHUNK), 0)
            >= lax.broadcasted_iota(jnp.int32, (HG_CHUNK, HG_CHUNK), 1))
    amat_ref[...] = jnp.zeros_like(amat_ref)
    for hd in range(N_HEADS):
        ls = slice(hd * HG_DK, (hd + 1) * HG_DK)
        for c in range(T // HG_CHUNK):
            rs = slice(c * HG_CHUNK, (c + 1) * HG_CHUNK)
            q = _silu(p_ref[rs, OFF_HQ + ls.start:OFF_HQ + ls.stop])
            bb = b_ref[rs, ls]
            blast = bb[HG_CHUNK - 1:HG_CHUNK, :]
            qt = q * jnp.exp(bb)
            kh = (k_ref[rs, ls] * jnp.exp(blast - bb)).astype(BF16)
            hg_ref[rs, HG_QT + ls.start:HG_QT + ls.stop] = qt.astype(BF16)
            kht_ref[c, ls, :] = kh.T
            hg_ref[rs, HG_V + ls.start:HG_V + ls.stop] = p_ref[rs, OFF_HI + ls.start:OFF_HI + ls.stop].astype(BF16)
            dec_ref[c, :, ls] = jnp.exp(blast)
            qh = (qt * jnp.exp(jnp.minimum(-blast, EXP_CLAMP))).astype(BF16)
            a = lax.dot_general(qh, kh, NT_DIMS, preferred_element_type=F32)
            amat_ref[rs, ls.start:ls.start + HG_CHUNK] = jnp.where(tril, a, 0.0).astype(BF16)


def _hgrn_scores_exact(p_ref, k_ref, b_ref, amat_ref):
    T = p_ref.shape[0]
    tril, col = _padded_tril()
    for hd in range(N_HEADS):
        ls = slice(hd * HG_DK, (hd + 1) * HG_DK)

        def chunk_body(c, carry):
            r0 = pl.multiple_of(c * HG_CHUNK, HG_CHUNK)
            rows = pl.ds(r0, HG_CHUNK)
            q = _silu(p_ref[rows, OFF_HQ + ls.start:OFF_HQ + ls.stop])
            bb = b_ref[rows, ls]

            def group_body(g, acc):
                base = pl.multiple_of(r0 + g * SUBLANES, SUBLANES)
                kg = k_ref[pl.ds(base, SUBLANES), ls]
                bg = b_ref[pl.ds(base, SUBLANES), ls]
                for j in range(SUBLANES):
                    e = jnp.exp(jnp.minimum(bb - bg[j:j + 1, :], 0.0))
                    colv = jnp.sum(q * kg[j:j + 1, :] * e, axis=-1, keepdims=True)
                    acc = jnp.where(col == g * SUBLANES + j, colv, acc)
                return acc

            a = lax.fori_loop(0, HG_CHUNK // SUBLANES, group_body, jnp.zeros((HG_CHUNK, LANES), F32))
            amat_ref[rows, ls] = jnp.where(tril, a, 0.0).astype(BF16)
            return carry

        lax.fori_loop(0, T // HG_CHUNK, chunk_body, 0)


def _retention_operands(p_ref, cos_ref, sina_ref, sinb_ref, xi_ref, zeta_ref, ret_ref):
    cos, sina, sinb = cos_ref[...], sina_ref[...], sinb_ref[...]

    def rot(u):
        return u * cos + pltpu.roll(u, 2 * RET_DK - RET_DK // 2, 1) * sina + pltpu.roll(u, RET_DK // 2, 1) * sinb

    for pr in range(N_PAIRS):
        qk = slice(pr * 2 * RET_DK, (pr + 1) * 2 * RET_DK)
        qr = rot(p_ref[:, OFF_RQ + qk.start:OFF_RQ + qk.stop])
        kr = rot(p_ref[:, OFF_RK + qk.start:OFF_RK + qk.stop]) * (RET_DK ** -0.5)
        ret_ref[:, RT_Q + qk.start:RT_Q + qk.stop] = qr.astype(BF16)
        ret_ref[:, RT_K + qk.start:RT_K + qk.stop] = kr.astype(BF16)
        ret_ref[:, RT_KZ + qk.start:RT_KZ + qk.stop] = (kr * zeta_ref[pr]).astype(BF16)
        ret_ref[:, RT_QXI + qk.start:RT_QXI + qk.stop] = (qr * xi_ref[pr]).astype(BF16)
    ret_ref[:, RT_V:RT_V + N_HEADS * HEAD_DV] = p_ref[:, OFF_RV:OFF_RV + N_HEADS * HEAD_DV].astype(BF16)


def _front_kernel(x_ref, mod_ref, ng_ref, win_ref, lb_ref, ltri_ref, cos_ref, sina_ref, sinb_ref,
                  xi_ref, zeta_ref,
                  hg_ref, kht_ref, ret_ref, gate_ref, amat_ref, dec_ref,
                  p_ref, hb_ref, k_ref, b_ref, lhi_ref, llo_ref):
    _normalize(x_ref[...], mod_ref[...], ng_ref, hb_ref)
    proj = _ProjectionStream(
        hb_ref, win_ref, p_ref,
        _pieces(OFF_HF, 1024) + _pieces(OFF_HZ, 1024) + _pieces(OFF_GA, 1024)
        + _pieces(OFF_RZ, 1024) + _pieces(OFF_GB, 1024) + _pieces(OFF_RQ, 1024) + _pieces(OFF_RV, 1024)
        + _pieces(OFF_HQ, 1024) + _pieces(OFF_HI, 1024))
    proj.emit(4)
    _forget_gates(p_ref, lb_ref, k_ref, lhi_ref, llo_ref, proj.emit)
    ltri = ltri_ref[...]
    b_ref[...] = _dot(ltri, lhi_ref[...]) + _dot(ltri, llo_ref[...])
    _gate_product(p_ref, OFF_GA, OFF_HZ, gate_ref, GT_A, proj.emit)
    _gate_product(p_ref, OFF_GB, OFF_RZ, gate_ref, GT_B, proj.emit)
    proj.emit(8)
    _retention_operands(p_ref, cos_ref, sina_ref, sinb_ref, xi_ref, zeta_ref, ret_ref)
    _hgrn_operands(p_ref, k_ref, b_ref, hg_ref, kht_ref, amat_ref, dec_ref)

    worst = jnp.float32(0.0)
    for c in range(x_ref.shape[0] // HG_CHUNK):
        worst = jnp.maximum(worst, jnp.max(-b_ref[(c + 1) * HG_CHUNK - 1:(c + 1) * HG_CHUNK, :]))

    @pl.when(worst > EXP_CLAMP)
    def _():
        _hgrn_scores_exact(p_ref, k_ref, b_ref, amat_ref)


def _hgrn_recurrence(hg_ref, kht_ref, amat_ref, dec_ref, s_ref, oa_ref):
    T = hg_ref.shape[0]
    states = [s_ref[hd] for hd in range(N_HEADS)]
    for c in range(T // HG_CHUNK):
        rs = slice(c * HG_CHUNK, (c + 1) * HG_CHUNK)
        for hd in range(N_HEADS):
            ls = slice(hd * HG_DK, (hd + 1) * HG_DK)
            qt = hg_ref[rs, HG_QT + ls.start:HG_QT + ls.stop]
            v = hg_ref[rs, HG_V + ls.start:HG_V + ls.stop]
            st = states[hd]
            intra = _dot(amat_ref[rs, ls.start:ls.start + HG_CHUNK], v)
            oa_ref[rs, ls] = intra + _dot(qt, st.astype(BF16))
            decay = jnp.transpose(jnp.broadcast_to(dec_ref[c, :, ls], (HEAD_DV, HG_DK)))
            states[hd] = st * decay + _dot(kht_ref[c, ls, :], v)
    for hd in range(N_HEADS):
        s_ref[hd] = states[hd]


def _retention_recurrence(ret_ref, dm_ref, rdec_ref, r_ref, ob_ref):
    T = ret_ref.shape[0]
    first_head = lax.broadcasted_iota(jnp.int32, (RET_CHUNK, 2 * RET_DK), 1) < RET_DK
    bd_row = lax.broadcasted_iota(jnp.int32, (2 * RET_DK, 2 * HEAD_DV), 0)
    bd_col = lax.broadcasted_iota(jnp.int32, (2 * RET_DK, 2 * HEAD_DV), 1)
    bd_mask = (bd_row < RET_DK) == (bd_col < HEAD_DV)
    states = [r_ref[pr] for pr in range(N_PAIRS)]
    for c in range(T // RET_CHUNK):
        rs = slice(c * RET_CHUNK, (c + 1) * RET_CHUNK)
        for pr in range(N_PAIRS):
            qk = slice(pr * 2 * RET_DK, (pr + 1) * 2 * RET_DK)
            vs = slice(pr * 2 * HEAD_DV, (pr + 1) * 2 * HEAD_DV)
            rbd = states[pr]
            qr = ret_ref[rs, RT_Q + qk.start:RT_Q + qk.stop]
            kr = ret_ref[rs, RT_K + qk.start:RT_K + qk.stop]
            vpair = ret_ref[rs, RT_V + vs.start:RT_V + vs.stop]
            zero = jnp.zeros_like(qr)
            q0 = jnp.where(first_head, qr, zero)
            q1 = jnp.where(first_head, zero, qr)
            sc = lax.dot_general(jnp.concatenate([q0, q1], axis=0), kr, NT_DIMS, preferred_element_type=F32)
            sc0 = sc[:RET_CHUNK] * dm_ref[2 * pr]
            sc1 = sc[RET_CHUNK:] * dm_ref[2 * pr + 1]
            intra0 = _dot(sc0.astype(BF16), vpair[:, :HEAD_DV])
            intra1 = _dot(sc1.astype(BF16), vpair[:, HEAD_DV:])
            cross = _dot(ret_ref[rs, RT_QXI + qk.start:RT_QXI + qk.stop], rbd.astype(BF16))
            ob_ref[rs, vs] = jnp.concatenate([intra0, intra1], axis=1) + cross
            cstate = lax.dot_general(ret_ref[rs, RT_KZ + qk.start:RT_KZ + qk.stop], vpair, TN_DIMS,
                                     preferred_element_type=F32)
            states[pr] = rbd * rdec_ref[pr] + jnp.where(bd_mask, cstate, 0.0)
    for pr in range(N_PAIRS):
        r_ref[pr] = states[pr]


def _merge_out(x, gate, oa_ref, ob_ref, gate_ref, hgg_ref, rtg_ref, fg_ref, wout_ref, m_ref, o_ref):
    def normed(o, g):
        return o * lax.rsqrt(jnp.mean(o * o, axis=-1, keepdims=True) + EPS) * g

    for hd in range(N_HEADS):
        ls = slice(hd * HEAD_DV, (hd + 1) * HEAD_DV)
        ua = normed(oa_ref[:, ls], hgg_ref[:, ls]) * gate_ref[:, GT_A + ls.start:GT_A + ls.stop]
        ub = normed(ob_ref[:, ls], rtg_ref[:, ls]) * gate_ref[:, GT_B + ls.start:GT_B + ls.stop]
        m_ref[:, ls] = (ua + ub).astype(BF16)
    xo = x + gate * _dot(m_ref[...], wout_ref[...])
    o_ref[...] = xo * lax.rsqrt(jnp.mean(xo * xo, axis=-1, keepdims=True) + EPS) * fg_ref[...]


def _back_kernel(x_ref, mod_ref, hg_ref, kht_ref, ret_ref, gate_ref, amat_ref, dec_ref, hgg_ref, rtg_ref, fg_ref,
                 wout_ref, dm_ref, rdec_ref,
                 o_ref,
                 s_ref, r_ref, oa_ref, ob_ref, m_ref):
    @pl.when(pl.program_id(1) == 0)
    def _():
        s_ref[...] = jnp.zeros_like(s_ref)
        r_ref[...] = jnp.zeros_like(r_ref)

    _hgrn_recurrence(hg_ref, kht_ref, amat_ref, dec_ref, s_ref, oa_ref)
    _retention_recurrence(ret_ref, dm_ref, rdec_ref, r_ref, ob_ref)
    _merge_out(x_ref[...], mod_ref[:, 2 * D_MODEL:], oa_ref, ob_ref, gate_ref, hgg_ref, rtg_ref, fg_ref,
               wout_ref, m_ref, o_ref)


def _rotary_tables(seq_len):
    half = RET_DK // 2
    inv_freq = 1.0 / (ROPE_BASE ** jnp.linspace(0.0, 1.0, half, dtype=F32))
    ang = jnp.arange(seq_len, dtype=jnp.int32).astype(F32)[:, None] * inv_freq[None, :]
    cos, sin = jnp.cos(ang), jnp.sin(ang)
    zero = jnp.zeros_like(sin)
    cos_t = jnp.tile(cos, (1, 4))
    sina_t = jnp.tile(jnp.concatenate([-sin, zero], axis=1), (1, 2))
    sinb_t = jnp.tile(jnp.concatenate([zero, sin], axis=1), (1, 2))
    return cos_t, sina_t, sinb_t


def _retention_tables(tile_t):
    c = RET_CHUNK
    log_gamma = jnp.log(1.0 - jnp.exp2(-5.0 - jnp.arange(N_HEADS, dtype=F32)))
    idx = jnp.arange(c, dtype=F32)
    rel = idx[:, None] - idx[None, :]
    dm = jnp.where(rel >= 0, jnp.exp(log_gamma[:, None, None] * jnp.maximum(rel, 0.0)), 0.0)
    zeta = jnp.exp(log_gamma[:, None] * (c - 1.0 - idx))
    xi = jnp.exp(log_gamma[:, None] * (idx + 1.0))
    decay = jnp.exp(log_gamma * c)

    def per_pair_lanes(t):
        t = jnp.transpose(t.reshape(N_PAIRS, 2, c), (0, 2, 1))
        return jnp.tile(jnp.repeat(t, RET_DK, axis=2), (1, tile_t // c, 1))

    dec = jnp.repeat(decay.reshape(N_PAIRS, 1, 2), HEAD_DV, axis=2)
    return dm, per_pair_lanes(xi), per_pair_lanes(zeta), dec


def _chunk_tril(tile_t):
    r = jnp.arange(tile_t, dtype=jnp.int32)
    same_chunk = (r[:, None] // HG_CHUNK) == (r[None, :] // HG_CHUNK)
    return jnp.where(same_chunk & (r[:, None] >= r[None, :]), 1.0, 0.0).astype(BF16)


def _const_spec(shape, single_buffer=False):
    zeros = (0,) * len(shape)
    if single_buffer:
        return pl.BlockSpec(shape, lambda b, t: zeros, pipeline_mode=pl.Buffered(1))
    return pl.BlockSpec(shape, lambda b, t: zeros)


def _prep_call(c, w_ada, b_ada, lb_logits):
    bsz = c.shape[0]
    return pl.pallas_call(
        _prep_kernel,
        out_shape=(jax.ShapeDtypeStruct((bsz, 3 * D_MODEL), F32),
                   jax.ShapeDtypeStruct((1, N_HEADS * HG_DK), F32)),
        compiler_params=pltpu.CompilerParams(vmem_limit_bytes=V7X_VMEM_LIMIT_BYTES),
        name="adaln_prep",
    )(c, w_ada, b_ada.reshape(1, -1), lb_logits)


def _tile_spec(tile_t, width):
    return pl.BlockSpec((None, tile_t, width), lambda b, t: (b, t, 0))


def _front_call(x, mod3, norm_g, w_in, lb, tile_t):
    bsz, seq, d = x.shape
    n_hc = tile_t // HG_CHUNK
    cos_t, sina_t, sinb_t = _rotary_tables(seq)
    _, xi, zeta, _ = _retention_tables(tile_t)
    ltri = _chunk_tril(tile_t)
    rot_spec = pl.BlockSpec((tile_t, 2 * RET_DK), lambda b, t: (t, 0))
    dec_spec = pl.BlockSpec((None, n_hc, 1, d), lambda b, t: (b, t, 0, 0))
    kht_spec = pl.BlockSpec((None, n_hc, d, HG_CHUNK), lambda b, t: (b, t, 0, 0))
    bf = lambda width: jax.ShapeDtypeStruct((bsz, seq, width), BF16)
    return pl.pallas_call(
        _front_kernel,
        grid=(bsz, seq // tile_t),
        in_specs=[
            _tile_spec(tile_t, d),
            pl.BlockSpec((None, 1, 3 * d), lambda b, t: (b, 0, 0)),
            _const_spec((1, d)),
            _const_spec((d, D_IN), single_buffer=True),
            _const_spec((1, d)),
            _const_spec(ltri.shape),
            rot_spec, rot_spec, rot_spec,
            _const_spec(xi.shape), _const_spec(zeta.shape),
        ],
        out_specs=[
            _tile_spec(tile_t, HG_WIDTH), kht_spec, _tile_spec(tile_t, RT_WIDTH), _tile_spec(tile_t, GT_WIDTH),
            _tile_spec(tile_t, d), dec_spec,
        ],
        out_shape=[bf(HG_WIDTH), jax.ShapeDtypeStruct((bsz, seq // HG_CHUNK, d, HG_CHUNK), BF16),
                   bf(RT_WIDTH), bf(GT_WIDTH), bf(d),
                   jax.ShapeDtypeStruct((bsz, seq // HG_CHUNK, 1, d), F32)],
        scratch_shapes=[
            pltpu.VMEM((tile_t, D_IN), F32),
            pltpu.VMEM((tile_t, d), BF16),
            pltpu.VMEM((tile_t, d), F32),
            pltpu.VMEM((tile_t, d), F32),
            pltpu.VMEM((tile_t, d), BF16),
            pltpu.VMEM((tile_t, d), BF16),
        ],
        compiler_params=pltpu.CompilerParams(
            dimension_semantics=("arbitrary", "arbitrary"),
            vmem_limit_bytes=V7X_VMEM_LIMIT_BYTES),
        name="layer_front",
    )(x, mod3, norm_g.reshape(1, -1), w_in, lb, ltri, cos_t, sina_t, sinb_t, xi, zeta)


def _back_call(x, mod3, hg, kht, ret, gate, amat, dec, hg_g, ret_g, final_g, w_out, tile_t):
    bsz, seq, d = x.shape
    n_hc = tile_t // HG_CHUNK
    dm, _, _, rdec = _retention_tables(tile_t)
    row = lambda v: v.reshape(1, -1)
    return pl.pallas_call(
        _back_kernel,
        grid=(bsz, seq // tile_t),
        in_specs=[
            _tile_spec(tile_t, d),
            pl.BlockSpec((None, 1, 3 * d), lambda b, t: (b, 0, 0)),
            _tile_spec(tile_t, HG_WIDTH),
            pl.BlockSpec((None, n_hc, d, HG_CHUNK), lambda b, t: (b, t, 0, 0)),
            _tile_spec(tile_t, RT_WIDTH), _tile_spec(tile_t, GT_WIDTH),
            _tile_spec(tile_t, d),
            pl.BlockSpec((None, n_hc, 1, d), lambda b, t: (b, t, 0, 0)),
            _const_spec((1, d)), _const_spec((1, d)), _const_spec((1, d)),
            _const_spec((d, d), single_buffer=True),
            _const_spec(dm.shape), _const_spec(rdec.shape),
        ],
        out_specs=_tile_spec(tile_t, d),
        out_shape=jax.ShapeDtypeStruct(x.shape, x.dtype),
        scratch_shapes=[
            pltpu.VMEM((N_HEADS, HG_DK, HEAD_DV), F32),
            pltpu.VMEM((N_PAIRS, 2 * RET_DK, 2 * HEAD_DV), F32),
            pltpu.VMEM((tile_t, d), F32),
            pltpu.VMEM((tile_t, d), F32),
            pltpu.VMEM((tile_t, d), BF16),
        ],
        compiler_params=pltpu.CompilerParams(
            dimension_semantics=("arbitrary", "arbitrary"),
            vmem_limit_bytes=V7X_VMEM_LIMIT_BYTES),
        name="layer_back",
    )(x, mod3, hg, kht, ret, gate, amat, dec, row(hg_g), row(ret_g), row(final_g), w_out, dm, rdec)


def kernel(x, c, norm_g, w_ada, b_ada, w_in, hg_lb_logits, hg_norm_g, ret_norm_g, w_out, final_g):
    depth = norm_g.shape[0]
    assert depth == 1 and hg_lb_logits.shape[0] == 2
    bsz, seq, d = x.shape
    tile_t = min(TILE_T, seq)
    assert d == D_MODEL and seq % tile_t == 0 and tile_t % RET_CHUNK == 0
    mod, lb = _prep_call(c, w_ada[0], b_ada[0], hg_lb_logits)
    mod3 = mod.reshape(bsz, 1, 3 * d)
    hg, kht, ret, gate, amat, dec = _front_call(x, mod3, norm_g[0], w_in[0].astype(BF16), lb, tile_t)
    return _back_call(x, mod3, hg, kht, ret, gate, amat, dec, hg_norm_g[0], ret_norm_g[0], final_g,
                      w_out[0].astype(BF16), min(TILE_BACK, seq))
```

```python
import jax
import jax.numpy as jnp
from jax import lax
from jax.experimental import pallas as pl
from jax.experimental.pallas import tpu as pltpu

F32 = jnp.float32
BF16 = jnp.bfloat16

D_MODEL = 1024
N_HEADS = 8
HEAD_DV = 128
HG_DK = 128
RET_DK = 64
ROPE_BASE = 10000.0
EPS = 1e-6

TILE_T = 256
TILE_BACK = 512
HG_CHUNK = 64
RET_CHUNK = 128
N_PAIRS = N_HEADS // 2
SUBLANES = 8
LANES = 128

OFF_HQ, OFF_HF, OFF_HI, OFF_HZ = 0, 1024, 2048, 3072
OFF_RQ, OFF_RK, OFF_RV, OFF_RZ = 4096, 4608, 5120, 6144
OFF_GA, OFF_GB = 7168, 8192
D_IN = 9216
PROJ_BLOCK = 256
WEIGHT_STAGE_SLOTS = 4

HG_QT, HG_V, HG_WIDTH = 0, 1024, 2048
RT_Q, RT_K, RT_KZ, RT_QXI, RT_V, RT_WIDTH = 0, 512, 1024, 1536, 2048, 3072
GT_A, GT_B, GT_WIDTH = 0, 1024, 2048

EXP_CLAMP = 80.0

V7X_VMEM_LIMIT_BYTES = 58 * 1024 * 1024

NT_DIMS = (((1,), (1,)), ((), ()))
TN_DIMS = (((0,), (0,)), ((), ()))


def _sigmoid(x):
    return 0.5 * (jnp.tanh(0.5 * x) + 1.0)


def _silu(x):
    return x * _sigmoid(x)


def _dot(a, b):
    return jnp.dot(a, b, preferred_element_type=F32)


def _prep_kernel(c_ref, wada_ref, bada_ref, lbl_ref, mod_ref, lb_ref):
    c = c_ref[...]
    sc = _silu(c).astype(BF16)
    mod_ref[...] = _dot(sc, wada_ref[...].astype(BF16)) + bada_ref[...]
    lg = lbl_ref[...]
    l0, l1 = lg[0:1, :], lg[1:2, :]
    mx = jnp.maximum(l0, l1)
    e0, e1 = jnp.exp(l0 - mx), jnp.exp(l1 - mx)
    lb_ref[...] = e0 / (e0 + e1)


class _ProjectionStream:
    def __init__(self, hb_ref, win_ref, p_ref, pieces):
        self.hb_ref, self.win_ref, self.p_ref = hb_ref, win_ref, p_ref
        self.pieces = list(pieces)

    def emit(self, n_pieces=1):
        for _ in range(min(n_pieces, len(self.pieces))):
            cs = slice(self.pieces[0] * PROJ_BLOCK, (self.pieces[0] + 1) * PROJ_BLOCK)
            self.pieces.pop(0)
            self.p_ref[:, cs] = _dot(self.hb_ref[...], self.win_ref[:, cs])


def _load_projection_weight(w_hbm, stage_ref, sem, w_ref):
    n_pieces = w_ref.shape[1] // PROJ_BLOCK
    n_slots = stage_ref.shape[0]

    def copy(i):
        cols = pl.ds(i * PROJ_BLOCK, PROJ_BLOCK)
        return pltpu.make_async_copy(w_hbm.at[0, :, cols], stage_ref.at[i % n_slots], sem.at[i % n_slots])

    for i in range(min(n_slots, n_pieces)):
        copy(i).start()
    for i in range(n_pieces):
        copy(i).wait()
        w_ref[:, i * PROJ_BLOCK:(i + 1) * PROJ_BLOCK] = stage_ref[i % n_slots].astype(BF16)
        if i + n_slots < n_pieces:
            copy(i + n_slots).start()


def _pieces(offset, width):
    return list(range(offset // PROJ_BLOCK, (offset + width) // PROJ_BLOCK))


def _normalize(x, mod, ng_ref, hb_ref):
    shift, scale = mod[:, :D_MODEL], mod[:, D_MODEL:2 * D_MODEL]
    ms = jnp.mean(x * x, axis=-1, keepdims=True)
    h = x * lax.rsqrt(ms + EPS) * ng_ref[...] * (1.0 + scale) + shift
    hb_ref[...] = h.astype(BF16)


def _forget_gates(p_ref, lb_ref, k_ref, lhi_ref, llo_ref, tick):
    for hd in range(N_HEADS):
        ls = slice(hd * HG_DK, (hd + 1) * HG_DK)
        lbh = lb_ref[:, ls]
        f = lbh + (1.0 - lbh) * _sigmoid(p_ref[:, OFF_HF + ls.start:OFF_HF + ls.stop])
        k_ref[:, ls] = 1.0 - f
        logf = jnp.log(f)
        hi = logf.astype(BF16)
        lhi_ref[:, ls] = hi
        llo_ref[:, ls] = (logf - hi.astype(F32)).astype(BF16)
        tick()


def _gate_product(p_ref, g_off, z_off, gate_ref, out_off, tick):
    for hd in range(N_HEADS):
        ls = slice(hd * HEAD_DV, (hd + 1) * HEAD_DV)
        g = _sigmoid(p_ref[:, g_off + ls.start:g_off + ls.stop])
        z = _silu(p_ref[:, z_off + ls.start:z_off + ls.stop])
        gate_ref[:, out_off + ls.start:out_off + ls.stop] = (g * z).astype(BF16)
        tick()


def _padded_tril():
    row = lax.broadcasted_iota(jnp.int32, (HG_CHUNK, LANES), 0)
    col = lax.broadcasted_iota(jnp.int32, (HG_CHUNK, LANES), 1)
    return row >= col, col


def _hgrn_operands(p_ref, k_ref, b_ref, hg_ref, kht_ref, amat_ref, dec_ref):
    T = p_ref.shape[0]
    tril = (lax.broadcasted_iota(jnp.int32, (HG_CHUNK, HG_CHUNK), 0)
            >= lax.broadcasted_iota(jnp.int32, (HG_CHUNK, HG_CHUNK), 1))
    amat_ref[...] = jnp.zeros_like(amat_ref)
    for hd in range(N_HEADS):
        ls = slice(hd * HG_DK, (hd + 1) * HG_DK)
        for c in range(T // HG_CHUNK):
            rs = slice(c * HG_CHUNK, (c + 1) * HG_CHUNK)
            q = _silu(p_ref[rs, OFF_HQ + ls.start:OFF_HQ + ls.stop])
            bb = b_ref[rs, ls]
            blast = bb[HG_CHUNK - 1:HG_CHUNK, :]
            qt = q * jnp.exp(bb)
            kh = (k_ref[rs, ls] * jnp.exp(blast - bb)).astype(BF16)
            hg_ref[rs, HG_QT + ls.start:HG_QT + ls.stop] = qt.astype(BF16)
            kht_ref[c, ls, :] = kh.T
            hg_ref[rs, HG_V + ls.start:HG_V + ls.stop] = p_ref[rs, OFF_HI + ls.start:OFF_HI + ls.stop].astype(BF16)
            dec_ref[c, :, ls] = jnp.exp(blast)
            qh = (qt * jnp.exp(jnp.minimum(-blast, EXP_CLAMP))).astype(BF16)
            a = lax.dot_general(qh, kh, NT_DIMS, preferred_element_type=F32)
            amat_ref[rs, ls.start:ls.start + HG_CHUNK] = jnp.where(tril, a, 0.0).astype(BF16)


def _hgrn_scores_exact(p_ref, k_ref, b_ref, amat_ref):
    T = p_ref.shape[0]
    tril, col = _padded_tril()
    for hd in range(N_HEADS):
        ls = slice(hd * HG_DK, (hd + 1) * HG_DK)

        def chunk_body(c, carry):
            r0 = pl.multiple_of(c * HG_CHUNK, HG_CHUNK)
            rows = pl.ds(r0, HG_CHUNK)
            q = _silu(p_ref[rows, OFF_HQ + ls.start:OFF_HQ + ls.stop])
            bb = b_ref[rows, ls]

            def group_body(g, acc):
                base = pl.multiple_of(r0 + g * SUBLANES, SUBLANES)
                kg = k_ref[pl.ds(base, SUBLANES), ls]
                bg = b_ref[pl.ds(base, SUBLANES), ls]
                for j in range(SUBLANES):
                    e = jnp.exp(jnp.minimum(bb - bg[j:j + 1, :], 0.0))
                    colv = jnp.sum(q * kg[j:j + 1, :] * e, axis=-1, keepdims=True)
                    acc = jnp.where(col == g * SUBLANES + j, colv, acc)
                return acc

            a = lax.fori_loop(0, HG_CHUNK // SUBLANES, group_body, jnp.zeros((HG_CHUNK, LANES), F32))
            amat_ref[rows, ls] = jnp.where(tril, a, 0.0).astype(BF16)
            return carry

        lax.fori_loop(0, T // HG_CHUNK, chunk_body, 0)


def _retention_operands(p_ref, cos_ref, sina_ref, sinb_ref, xi_ref, zeta_ref, ret_ref):
    cos, sina, sinb = cos_ref[...], sina_ref[...], sinb_ref[...]

    def rot(u):
        return u * cos + pltpu.roll(u, 2 * RET_DK - RET_DK // 2, 1) * sina + pltpu.roll(u, RET_DK // 2, 1) * sinb

    for pr in range(N_PAIRS):
        qk = slice(pr * 2 * RET_DK, (pr + 1) * 2 * RET_DK)
        qr = rot(p_ref[:, OFF_RQ + qk.start:OFF_RQ + qk.stop])
        kr = rot(p_ref[:, OFF_RK + qk.start:OFF_RK + qk.stop]) * (RET_DK ** -0.5)
        ret_ref[:, RT_Q + qk.start:RT_Q + qk.stop] = qr.astype(BF16)
        ret_ref[:, RT_K + qk.start:RT_K + qk.stop] = kr.astype(BF16)
        ret_ref[:, RT_KZ + qk.start:RT_KZ + qk.stop] = (kr * zeta_ref[pr]).astype(BF16)
        ret_ref[:, RT_QXI + qk.start:RT_QXI + qk.stop] = (qr * xi_ref[pr]).astype(BF16)
    ret_ref[:, RT_V:RT_V + N_HEADS * HEAD_DV] = p_ref[:, OFF_RV:OFF_RV + N_HEADS * HEAD_DV].astype(BF16)


def _front_kernel(x_ref, mod_ref, ng_ref, win_ref, lb_ref, ltri_ref, cos_ref, sina_ref, sinb_ref,
                  xi_ref, zeta_ref,
                  hg_ref, kht_ref, ret_ref, gate_ref, amat_ref, dec_ref,
                  p_ref, hb_ref, k_ref, b_ref, lhi_ref, llo_ref, wbf_ref, stage_ref, sem):
    @pl.when((pl.program_id(0) == 0) & (pl.program_id(1) == 0))
    def _():
        _load_projection_weight(win_ref, stage_ref, sem, wbf_ref)

    _normalize(x_ref[...], mod_ref[...], ng_ref, hb_ref)
    proj = _ProjectionStream(
        hb_ref, wbf_ref, p_ref,
        _pieces(OFF_HF, 1024) + _pieces(OFF_HZ, 1024) + _pieces(OFF_GA, 1024)
        + _pieces(OFF_RZ, 1024) + _pieces(OFF_GB, 1024) + _pieces(OFF_RQ, 1024) + _pieces(OFF_RV, 1024)
        + _pieces(OFF_HQ, 1024) + _pieces(OFF_HI, 1024))
    proj.emit(4)
    _forget_gates(p_ref, lb_ref, k_ref, lhi_ref, llo_ref, proj.emit)
    ltri = ltri_ref[...]
    b_ref[...] = _dot(ltri, lhi_ref[...]) + _dot(ltri, llo_ref[...])
    _gate_product(p_ref, OFF_GA, OFF_HZ, gate_ref, GT_A, proj.emit)
    _gate_product(p_ref, OFF_GB, OFF_RZ, gate_ref, GT_B, proj.emit)
    proj.emit(8)
    _retention_operands(p_ref, cos_ref, sina_ref, sinb_ref, xi_ref, zeta_ref, ret_ref)
    _hgrn_operands(p_ref, k_ref, b_ref, hg_ref, kht_ref, amat_ref, dec_ref)

    worst = jnp.float32(0.0)
    for c in range(x_ref.shape[0] // HG_CHUNK):
        worst = jnp.maximum(worst, jnp.max(-b_ref[(c + 1) * HG_CHUNK - 1:(c + 1) * HG_CHUNK, :]))

    @pl.when(worst > EXP_CLAMP)
    def _():
        _hgrn_scores_exact(p_ref, k_ref, b_ref, amat_ref)


def _hgrn_recurrence(hg_ref, kht_ref, amat_ref, dec_ref, s_ref, oa_ref):
    T = hg_ref.shape[0]
    states = [s_ref[hd] for hd in range(N_HEADS)]
    for c in range(T // HG_CHUNK):
        rs = slice(c * HG_CHUNK, (c + 1) * HG_CHUNK)
        for hd in range(N_HEADS):
            ls = slice(hd * HG_DK, (hd + 1) * HG_DK)
            qt = hg_ref[rs, HG_QT + ls.start:HG_QT + ls.stop]
            v = hg_ref[rs, HG_V + ls.start:HG_V + ls.stop]
            st = states[hd]
            intra = _dot(amat_ref[rs, ls.start:ls.start + HG_CHUNK], v)
            oa_ref[rs, ls] = intra + _dot(qt, st.astype(BF16))
            decay = jnp.transpose(jnp.broadcast_to(dec_ref[c, :, ls], (HEAD_DV, HG_DK)))
            states[hd] = st * decay + _dot(kht_ref[c, ls, :], v)
    for hd in range(N_HEADS):
        s_ref[hd] = states[hd]


def _retention_recurrence(ret_ref, dm_ref, rdec_ref, r_ref, ob_ref):
    T = ret_ref.shape[0]
    first_head = lax.broadcasted_iota(jnp.int32, (RET_CHUNK, 2 * RET_DK), 1) < RET_DK
    bd_row = lax.broadcasted_iota(jnp.int32, (2 * RET_DK, 2 * HEAD_DV), 0)
    bd_col = lax.broadcasted_iota(jnp.int32, (2 * RET_DK, 2 * HEAD_DV), 1)
    bd_mask = (bd_row < RET_DK) == (bd_col < HEAD_DV)
    states = [r_ref[pr] for pr in range(N_PAIRS)]
    for c in range(T // RET_CHUNK):
        rs = slice(c * RET_CHUNK, (c + 1) * RET_CHUNK)
        for pr in range(N_PAIRS):
            qk = slice(pr * 2 * RET_DK, (pr + 1) * 2 * RET_DK)
            vs = slice(pr * 2 * HEAD_DV, (pr + 1) * 2 * HEAD_DV)
            rbd = states[pr]
            qr = ret_ref[rs, RT_Q + qk.start:RT_Q + qk.stop]
            kr = ret_ref[rs, RT_K + qk.start:RT_K + qk.stop]
            vpair = ret_ref[rs, RT_V + vs.start:RT_V + vs.stop]
            zero = jnp.zeros_like(qr)
            q0 = jnp.where(first_head, qr, zero)
            q1 = jnp.where(first_head, zero, qr)
            sc = lax.dot_general(jnp.concatenate([q0, q1], axis=0), kr, NT_DIMS, preferred_element_type=F32)
            sc0 = sc[:RET_CHUNK] * dm_ref[2 * pr]
            sc1 = sc[RET_CHUNK:] * dm_ref[2 * pr + 1]
            intra0 = _dot(sc0.astype(BF16), vpair[:, :HEAD_DV])
            intra1 = _dot(sc1.astype(BF16), vpair[:, HEAD_DV:])
            cross = _dot(ret_ref[rs, RT_QXI + qk.start:RT_QXI + qk.stop], rbd.astype(BF16))
            ob_ref[rs, vs] = jnp.concatenate([intra0, intra1], axis=1) + cross
            cstate = lax.dot_general(ret_ref[rs, RT_KZ + qk.start:RT_KZ + qk.stop], vpair, TN_DIMS,
                                     preferred_element_type=F32)
            states[pr] = rbd * rdec_ref[pr] + jnp.where(bd_mask, cstate, 0.0)
    for pr in range(N_PAIRS):
        r_ref[pr] = states[pr]


def _merge_out(x, gate, oa_ref, ob_ref, gate_ref, hgg_ref, rtg_ref, fg_ref, wout_ref, m_ref, o_ref):
    def normed(o, g):
        return o * lax.rsqrt(jnp.mean(o * o, axis=-1, keepdims=True) + EPS) * g

    for hd in range(N_HEADS):
        ls = slice(hd * HEAD_DV, (hd + 1) * HEAD_DV)
        ua = normed(oa_ref[:, ls], hgg_ref[:, ls]) * gate_ref[:, GT_A + ls.start:GT_A + ls.stop]
        ub = normed(ob_ref[:, ls], rtg_ref[:, ls]) * gate_ref[:, GT_B + ls.start:GT_B + ls.stop]
        m_ref[:, ls] = (ua + ub).astype(BF16)
    xo = x + gate * _dot(m_ref[...], wout_ref[...])
    o_ref[...] = xo * lax.rsqrt(jnp.mean(xo * xo, axis=-1, keepdims=True) + EPS) * fg_ref[...]


def _back_kernel(x_ref, mod_ref, hg_ref, kht_ref, ret_ref, gate_ref, amat_ref, dec_ref, hgg_ref, rtg_ref, fg_ref,
                 wout_ref, dm_ref, rdec_ref,
                 o_ref,
                 s_ref, r_ref, oa_ref, ob_ref, m_ref):
    @pl.when(pl.program_id(1) == 0)
    def _():
        s_ref[...] = jnp.zeros_like(s_ref)
        r_ref[...] = jnp.zeros_like(r_ref)

    _hgrn_recurrence(hg_ref, kht_ref, amat_ref, dec_ref, s_ref, oa_ref)
    _retention_recurrence(ret_ref, dm_ref, rdec_ref, r_ref, ob_ref)
    _merge_out(x_ref[...], mod_ref[:, 2 * D_MODEL:], oa_ref, ob_ref, gate_ref, hgg_ref, rtg_ref, fg_ref,
               wout_ref, m_ref, o_ref)


def _rotary_tables(seq_len):
    half = RET_DK // 2
    inv_freq = 1.0 / (ROPE_BASE ** jnp.linspace(0.0, 1.0, half, dtype=F32))
    ang = jnp.arange(seq_len, dtype=jnp.int32).astype(F32)[:, None] * inv_freq[None, :]
    cos, sin = jnp.cos(ang), jnp.sin(ang)
    zero = jnp.zeros_like(sin)
    cos_t = jnp.tile(cos, (1, 4))
    sina_t = jnp.tile(jnp.concatenate([-sin, zero], axis=1), (1, 2))
    sinb_t = jnp.tile(jnp.concatenate([zero, sin], axis=1), (1, 2))
    return cos_t, sina_t, sinb_t


def _retention_tables(tile_t):
    c = RET_CHUNK
    log_gamma = jnp.log(1.0 - jnp.exp2(-5.0 - jnp.arange(N_HEADS, dtype=F32)))
    idx = jnp.arange(c, dtype=F32)
    rel = idx[:, None] - idx[None, :]
    dm = jnp.where(rel >= 0, jnp.exp(log_gamma[:, None, None] * jnp.maximum(rel, 0.0)), 0.0)
    zeta = jnp.exp(log_gamma[:, None] * (c - 1.0 - idx))
    xi = jnp.exp(log_gamma[:, None] * (idx + 1.0))
    decay = jnp.exp(log_gamma * c)

    def per_pair_lanes(t):
        t = jnp.transpose(t.reshape(N_PAIRS, 2, c), (0, 2, 1))
        return jnp.tile(jnp.repeat(t, RET_DK, axis=2), (1, tile_t // c, 1))

    dec = jnp.repeat(decay.reshape(N_PAIRS, 1, 2), HEAD_DV, axis=2)
    return dm, per_pair_lanes(xi), per_pair_lanes(zeta), dec


def _chunk_tril(tile_t):
    r = jnp.arange(tile_t, dtype=jnp.int32)
    same_chunk = (r[:, None] // HG_CHUNK) == (r[None, :] // HG_CHUNK)
    return jnp.where(same_chunk & (r[:, None] >= r[None, :]), 1.0, 0.0).astype(BF16)


def _const_spec(shape, single_buffer=False):
    zeros = (0,) * len(shape)
    if single_buffer:
        return pl.BlockSpec(shape, lambda b, t: zeros, pipeline_mode=pl.Buffered(1))
    return pl.BlockSpec(shape, lambda b, t: zeros)


def _prep_call(c, w_ada, b_ada, lb_logits):
    bsz = c.shape[0]
    return pl.pallas_call(
        _prep_kernel,
        out_shape=(jax.ShapeDtypeStruct((bsz, 3 * D_MODEL), F32),
                   jax.ShapeDtypeStruct((1, N_HEADS * HG_DK), F32)),
        compiler_params=pltpu.CompilerParams(vmem_limit_bytes=V7X_VMEM_LIMIT_BYTES),
        name="adaln_prep",
    )(c, w_ada, b_ada.reshape(1, -1), lb_logits)


def _tile_spec(tile_t, width):
    return pl.BlockSpec((None, tile_t, width), lambda b, t: (b, t, 0))


def _front_call(x, mod3, norm_g, w_in, lb, tile_t):
    bsz, seq, d = x.shape
    n_hc = tile_t // HG_CHUNK
    cos_t, sina_t, sinb_t = _rotary_tables(seq)
    _, xi, zeta, _ = _retention_tables(tile_t)
    ltri = _chunk_tril(tile_t)
    rot_spec = pl.BlockSpec((tile_t, 2 * RET_DK), lambda b, t: (t, 0))
    dec_spec = pl.BlockSpec((None, n_hc, 1, d), lambda b, t: (b, t, 0, 0))
    kht_spec = pl.BlockSpec((None, n_hc, d, HG_CHUNK), lambda b, t: (b, t, 0, 0))
    bf = lambda width: jax.ShapeDtypeStruct((bsz, seq, width), BF16)
    return pl.pallas_call(
        _front_kernel,
        grid=(bsz, seq // tile_t),
        in_specs=[
            _tile_spec(tile_t, d),
            pl.BlockSpec((None, 1, 3 * d), lambda b, t: (b, 0, 0)),
            _const_spec((1, d)),
            pl.BlockSpec(memory_space=pl.ANY),
            _const_spec((1, d)),
            _const_spec(ltri.shape),
            rot_spec, rot_spec, rot_spec,
            _const_spec(xi.shape), _const_spec(zeta.shape),
        ],
        out_specs=[
            _tile_spec(tile_t, HG_WIDTH), kht_spec, _tile_spec(tile_t, RT_WIDTH), _tile_spec(tile_t, GT_WIDTH),
            _tile_spec(tile_t, d), dec_spec,
        ],
        out_shape=[bf(HG_WIDTH), jax.ShapeDtypeStruct((bsz, seq // HG_CHUNK, d, HG_CHUNK), BF16),
                   bf(RT_WIDTH), bf(GT_WIDTH), bf(d),
                   jax.ShapeDtypeStruct((bsz, seq // HG_CHUNK, 1, d), F32)],
        scratch_shapes=[
            pltpu.VMEM((tile_t, D_IN), F32),
            pltpu.VMEM((tile_t, d), BF16),
            pltpu.VMEM((tile_t, d), F32),
            pltpu.VMEM((tile_t, d), F32),
            pltpu.VMEM((tile_t, d), BF16),
            pltpu.VMEM((tile_t, d), BF16),
            pltpu.VMEM((d, D_IN), BF16),
            pltpu.VMEM((WEIGHT_STAGE_SLOTS, d, PROJ_BLOCK), F32),
            pltpu.SemaphoreType.DMA((WEIGHT_STAGE_SLOTS,)),
        ],
        compiler_params=pltpu.CompilerParams(
            dimension_semantics=("arbitrary", "arbitrary"),
            vmem_limit_bytes=V7X_VMEM_LIMIT_BYTES),
        name="layer_front",
    )(x, mod3, norm_g.reshape(1, -1), w_in, lb, ltri, cos_t, sina_t, sinb_t, xi, zeta)


def _back_call(x, mod3, hg, kht, ret, gate, amat, dec, hg_g, ret_g, final_g, w_out, tile_t):
    bsz, seq, d = x.shape
    n_hc = tile_t // HG_CHUNK
    dm, _, _, rdec = _retention_tables(tile_t)
    row = lambda v: v.reshape(1, -1)
    return pl.pallas_call(
        _back_kernel,
        grid=(bsz, seq // tile_t),
        in_specs=[
            _tile_spec(tile_t, d),
            pl.BlockSpec((None, 1, 3 * d), lambda b, t: (b, 0, 0)),
            _tile_spec(tile_t, HG_WIDTH),
            pl.BlockSpec((None, n_hc, d, HG_CHUNK), lambda b, t: (b, t, 0, 0)),
            _tile_spec(tile_t, RT_WIDTH), _tile_spec(tile_t, GT_WIDTH),
            _tile_spec(tile_t, d),
            pl.BlockSpec((None, n_hc, 1, d), lambda b, t: (b, t, 0, 0)),
            _const_spec((1, d)), _const_spec((1, d)), _const_spec((1, d)),
            _const_spec((d, d), single_buffer=True),
            _const_spec(dm.shape), _const_spec(rdec.shape),
        ],
        out_specs=_tile_spec(tile_t, d),
        out_shape=jax.ShapeDtypeStruct(x.shape, x.dtype),
        scratch_shapes=[
            pltpu.VMEM((N_HEADS, HG_DK, HEAD_DV), F32),
            pltpu.VMEM((N_PAIRS, 2 * RET_DK, 2 * HEAD_DV), F32),
            pltpu.VMEM((tile_t, d), F32),
            pltpu.VMEM((tile_t, d), F32),
            pltpu.VMEM((tile_t, d), BF16),
        ],
        compiler_params=pltpu.CompilerParams(
            dimension_semantics=("arbitrary", "arbitrary"),
            vmem_limit_bytes=V7X_VMEM_LIMIT_BYTES),
        name="layer_back",
    )(x, mod3, hg, kht, ret, gate, amat, dec, row(hg_g), row(ret_g), row(final_g), w_out, dm, rdec)


def kernel(x, c, norm_g, w_ada, b_ada, w_in, hg_lb_logits, hg_norm_g, ret_norm_g, w_out, final_g):
    depth = norm_g.shape[0]
    assert depth == 1 and hg_lb_logits.shape[0] == 2
    bsz, seq, d = x.shape
    tile_t = min(TILE_T, seq)
    assert d == D_MODEL and seq % tile_t == 0 and tile_t % RET_CHUNK == 0
    mod, lb = _prep_call(c, w_ada[0], b_ada[0], hg_lb_logits)
    mod3 = mod.reshape(bsz, 1, 3 * d)
    hg, kht, ret, gate, amat, dec = _front_call(x, mod3, norm_g[0], w_in, lb, tile_t)
    return _back_call(x, mod3, hg, kht, ret, gate, amat, dec, hg_norm_g[0], ret_norm_g[0], final_g,
                      w_out[0].astype(BF16), min(TILE_BACK, seq))
```

```python
import functools

import jax
import jax.numpy as jnp
from jax import lax
from jax.experimental import pallas as pl
from jax.experimental.pallas import tpu as pltpu

F32 = jnp.float32
BF16 = jnp.bfloat16

D_MODEL = 1024
N_HEADS = 8
HEAD_DV = 128
HG_DK = 128
RET_DK = 64
ROPE_BASE = 10000.0
EPS = 1e-6

TILE_T = 256
TILE_BACK = 512
HG_CHUNK = 64
RET_CHUNK = 128
N_PAIRS = N_HEADS // 2
SUBLANES = 8
LANES = 128

OFF_HQ, OFF_HF, OFF_HI, OFF_HZ = 0, 1024, 2048, 3072
OFF_RQ, OFF_RK, OFF_RV, OFF_RZ = 4096, 4608, 5120, 6144
OFF_GA, OFF_GB = 7168, 8192
D_IN = 9216
PROJ_BLOCK = 256
WEIGHT_STAGE_SLOTS = 4

HG_QT, HG_V, HG_WIDTH = 0, 1024, 2048
RT_Q, RT_K, RT_KZ, RT_QXI, RT_V, RT_WIDTH = 0, 512, 1024, 1536, 2048, 3072
GT_A, GT_B, GT_WIDTH = 0, 1024, 2048

EXP_CLAMP = 80.0

V7X_VMEM_LIMIT_BYTES = 58 * 1024 * 1024

NT_DIMS = (((1,), (1,)), ((), ()))
TN_DIMS = (((0,), (0,)), ((), ()))


def _sigmoid(x):
    return 0.5 * (jnp.tanh(0.5 * x) + 1.0)


def _silu(x):
    return x * _sigmoid(x)


def _dot(a, b):
    return jnp.dot(a, b, preferred_element_type=F32)


def _prep_kernel(c_ref, wada_ref, bada_ref, lbl_ref, mod_ref, lb_ref):
    c = c_ref[...]
    sc = _silu(c).astype(BF16)
    mod_ref[...] = _dot(sc, wada_ref[...].astype(BF16)) + bada_ref[...]
    lg = lbl_ref[...]
    l0, l1 = lg[0:1, :], lg[1:2, :]
    mx = jnp.maximum(l0, l1)
    e0, e1 = jnp.exp(l0 - mx), jnp.exp(l1 - mx)
    lb_ref[...] = e0 / (e0 + e1)


class _ProjectionStream:
    def __init__(self, hb_ref, win_ref, p_ref, pieces):
        self.hb_ref, self.win_ref, self.p_ref = hb_ref, win_ref, p_ref
        self.pieces = list(pieces)

    def emit(self, n_pieces=1):
        for _ in range(min(n_pieces, len(self.pieces))):
            cs = slice(self.pieces[0] * PROJ_BLOCK, (self.pieces[0] + 1) * PROJ_BLOCK)
            self.pieces.pop(0)
            self.p_ref[:, cs] = _dot(self.hb_ref[...], self.win_ref[:, cs])


def _load_projection_weight(w_hbm, stage_ref, sem, w_ref):
    n_pieces = w_ref.shape[1] // PROJ_BLOCK
    n_slots = stage_ref.shape[0]

    def copy(i):
        cols = pl.ds(i * PROJ_BLOCK, PROJ_BLOCK)
        return pltpu.make_async_copy(w_hbm.at[0, :, cols], stage_ref.at[i % n_slots], sem.at[i % n_slots])

    for i in range(min(n_slots, n_pieces)):
        copy(i).start()
    for i in range(n_pieces):
        copy(i).wait()
        w_ref[:, i * PROJ_BLOCK:(i + 1) * PROJ_BLOCK] = stage_ref[i % n_slots].astype(BF16)
        if i + n_slots < n_pieces:
            copy(i + n_slots).start()


def _pieces(offset, width):
    return list(range(offset // PROJ_BLOCK, (offset + width) // PROJ_BLOCK))


def _normalize(x, mod, ng_ref, hb_ref):
    shift, scale = mod[:, :D_MODEL], mod[:, D_MODEL:2 * D_MODEL]
    ms = jnp.mean(x * x, axis=-1, keepdims=True)
    h = x * lax.rsqrt(ms + EPS) * ng_ref[...] * (1.0 + scale) + shift
    hb_ref[...] = h.astype(BF16)


def _forget_gates(p_ref, lb_ref, k_ref, lhi_ref, llo_ref, tick):
    for hd in range(N_HEADS):
        ls = slice(hd * HG_DK, (hd + 1) * HG_DK)
        lbh = lb_ref[:, ls]
        f = lbh + (1.0 - lbh) * _sigmoid(p_ref[:, OFF_HF + ls.start:OFF_HF + ls.stop])
        k_ref[:, ls] = 1.0 - f
        logf = jnp.log(f)
        hi = logf.astype(BF16)
        lhi_ref[:, ls] = hi
        llo_ref[:, ls] = (logf - hi.astype(F32)).astype(BF16)
        tick()


def _gate_product(p_ref, g_off, z_off, gate_ref, out_off, tick):
    for hd in range(N_HEADS):
        ls = slice(hd * HEAD_DV, (hd + 1) * HEAD_DV)
        g = _sigmoid(p_ref[:, g_off + ls.start:g_off + ls.stop])
        z = _silu(p_ref[:, z_off + ls.start:z_off + ls.stop])
        gate_ref[:, out_off + ls.start:out_off + ls.stop] = (g * z).astype(BF16)
        tick()


def _padded_tril():
    row = lax.broadcasted_iota(jnp.int32, (HG_CHUNK, LANES), 0)
    col = lax.broadcasted_iota(jnp.int32, (HG_CHUNK, LANES), 1)
    return row >= col, col


def _hgrn_operands(p_ref, k_ref, b_ref, hg_ref, kht_ref, amat_ref, dec_ref):
    T = p_ref.shape[0]
    tril = (lax.broadcasted_iota(jnp.int32, (HG_CHUNK, HG_CHUNK), 0)
            >= lax.broadcasted_iota(jnp.int32, (HG_CHUNK, HG_CHUNK), 1))
    amat_ref[...] = jnp.zeros_like(amat_ref)
    for hd in range(N_HEADS):
        ls = slice(hd * HG_DK, (hd + 1) * HG_DK)
        for c in range(T // HG_CHUNK):
            rs = slice(c * HG_CHUNK, (c + 1) * HG_CHUNK)
            q = _silu(p_ref[rs, OFF_HQ + ls.start:OFF_HQ + ls.stop])
            bb = b_ref[rs, ls]
            blast = bb[HG_CHUNK - 1:HG_CHUNK, :]
            qt = q * jnp.exp(bb)
            kh = (k_ref[rs, ls] * jnp.exp(blast - bb)).astype(BF16)
            hg_ref[rs, HG_QT + ls.start:HG_QT + ls.stop] = qt.astype(BF16)
            kht_ref[c, ls, :] = kh.T
            hg_ref[rs, HG_V + ls.start:HG_V + ls.stop] = p_ref[rs, OFF_HI + ls.start:OFF_HI + ls.stop].astype(BF16)
            dec_ref[c, :, ls] = jnp.exp(blast)
            qh = (qt * jnp.exp(jnp.minimum(-blast, EXP_CLAMP))).astype(BF16)
            a = lax.dot_general(qh, kh, NT_DIMS, preferred_element_type=F32)
            amat_ref[rs, ls.start:ls.start + HG_CHUNK] = jnp.where(tril, a, 0.0).astype(BF16)


def _hgrn_scores_exact(p_ref, k_ref, b_ref, amat_ref):
    T = p_ref.shape[0]
    tril, col = _padded_tril()
    for hd in range(N_HEADS):
        ls = slice(hd * HG_DK, (hd + 1) * HG_DK)

        def chunk_body(c, carry):
            r0 = pl.multiple_of(c * HG_CHUNK, HG_CHUNK)
            rows = pl.ds(r0, HG_CHUNK)
            q = _silu(p_ref[rows, OFF_HQ + ls.start:OFF_HQ + ls.stop])
            bb = b_ref[rows, ls]

            def group_body(g, acc):
                base = pl.multiple_of(r0 + g * SUBLANES, SUBLANES)
                kg = k_ref[pl.ds(base, SUBLANES), ls]
                bg = b_ref[pl.ds(base, SUBLANES), ls]
                for j in range(SUBLANES):
                    e = jnp.exp(jnp.minimum(bb - bg[j:j + 1, :], 0.0))
                    colv = jnp.sum(q * kg[j:j + 1, :] * e, axis=-1, keepdims=True)
                    acc = jnp.where(col == g * SUBLANES + j, colv, acc)
                return acc

            a = lax.fori_loop(0, HG_CHUNK // SUBLANES, group_body, jnp.zeros((HG_CHUNK, LANES), F32))
            amat_ref[rows, ls] = jnp.where(tril, a, 0.0).astype(BF16)
            return carry

        lax.fori_loop(0, T // HG_CHUNK, chunk_body, 0)


def _retention_operands(p_ref, cos_ref, sina_ref, sinb_ref, xi_ref, zeta_ref, ret_ref):
    cos, sina, sinb = cos_ref[...], sina_ref[...], sinb_ref[...]

    def rot(u):
        return u * cos + pltpu.roll(u, 2 * RET_DK - RET_DK // 2, 1) * sina + pltpu.roll(u, RET_DK // 2, 1) * sinb

    for pr in range(N_PAIRS):
        qk = slice(pr * 2 * RET_DK, (pr + 1) * 2 * RET_DK)
        qr = rot(p_ref[:, OFF_RQ + qk.start:OFF_RQ + qk.stop])
        kr = rot(p_ref[:, OFF_RK + qk.start:OFF_RK + qk.stop]) * (RET_DK ** -0.5)
        ret_ref[:, RT_Q + qk.start:RT_Q + qk.stop] = qr.astype(BF16)
        ret_ref[:, RT_K + qk.start:RT_K + qk.stop] = kr.astype(BF16)
        ret_ref[:, RT_KZ + qk.start:RT_KZ + qk.stop] = (kr * zeta_ref[pr]).astype(BF16)
        ret_ref[:, RT_QXI + qk.start:RT_QXI + qk.stop] = (qr * xi_ref[pr]).astype(BF16)
    ret_ref[:, RT_V:RT_V + N_HEADS * HEAD_DV] = p_ref[:, OFF_RV:OFF_RV + N_HEADS * HEAD_DV].astype(BF16)


def _front_kernel(x_hbm, mod_hbm, ng_hbm, win_hbm, lb_hbm, ltri_hbm, cos_hbm, sina_hbm, sinb_hbm,
                  xi_hbm, zeta_hbm,
                  hg_hbm, kht_hbm, ret_hbm, gate_hbm, amat_hbm, dec_hbm,
                  p_ref, hb_ref, k_ref, b_ref, lhi_ref, llo_ref, wbf_ref, stage_ref, sem,
                  *, grid, in_specs, out_specs):
    _load_projection_weight(win_hbm, stage_ref, sem, wbf_ref)
    step = functools.partial(_front_step, scratch=(p_ref, hb_ref, k_ref, b_ref, lhi_ref, llo_ref, wbf_ref))
    pltpu.emit_pipeline(step, grid=grid, in_specs=in_specs, out_specs=out_specs)(
        x_hbm, mod_hbm, ng_hbm, lb_hbm, ltri_hbm, cos_hbm, sina_hbm, sinb_hbm, xi_hbm, zeta_hbm,
        hg_hbm, kht_hbm, ret_hbm, gate_hbm, amat_hbm, dec_hbm)


def _front_step(x_ref, mod_ref, ng_ref, lb_ref, ltri_ref, cos_ref, sina_ref, sinb_ref, xi_ref, zeta_ref,
                hg_ref, kht_ref, ret_ref, gate_ref, amat_ref, dec_ref, *, scratch):
    p_ref, hb_ref, k_ref, b_ref, lhi_ref, llo_ref, wbf_ref = scratch
    _normalize(x_ref[...], mod_ref[...], ng_ref, hb_ref)
    proj = _ProjectionStream(
        hb_ref, wbf_ref, p_ref,
        _pieces(OFF_HF, 1024) + _pieces(OFF_HZ, 1024) + _pieces(OFF_GA, 1024)
        + _pieces(OFF_RZ, 1024) + _pieces(OFF_GB, 1024) + _pieces(OFF_RQ, 1024) + _pieces(OFF_RV, 1024)
        + _pieces(OFF_HQ, 1024) + _pieces(OFF_HI, 1024))
    proj.emit(4)
    _forget_gates(p_ref, lb_ref, k_ref, lhi_ref, llo_ref, proj.emit)
    ltri = ltri_ref[...]
    b_ref[...] = _dot(ltri, lhi_ref[...]) + _dot(ltri, llo_ref[...])
    _gate_product(p_ref, OFF_GA, OFF_HZ, gate_ref, GT_A, proj.emit)
    _gate_product(p_ref, OFF_GB, OFF_RZ, gate_ref, GT_B, proj.emit)
    proj.emit(8)
    _retention_operands(p_ref, cos_ref, sina_ref, sinb_ref, xi_ref, zeta_ref, ret_ref)
    _hgrn_operands(p_ref, k_ref, b_ref, hg_ref, kht_ref, amat_ref, dec_ref)

    worst = jnp.float32(0.0)
    for c in range(x_ref.shape[0] // HG_CHUNK):
        worst = jnp.maximum(worst, jnp.max(-b_ref[(c + 1) * HG_CHUNK - 1:(c + 1) * HG_CHUNK, :]))

    @pl.when(worst > EXP_CLAMP)
    def _():
        _hgrn_scores_exact(p_ref, k_ref, b_ref, amat_ref)


def _hgrn_recurrence(hg_ref, kht_ref, amat_ref, dec_ref, s_ref, oa_ref):
    T = hg_ref.shape[0]
    states = [s_ref[hd] for hd in range(N_HEADS)]
    for c in range(T // HG_CHUNK):
        rs = slice(c * HG_CHUNK, (c + 1) * HG_CHUNK)
        for hd in range(N_HEADS):
            ls = slice(hd * HG_DK, (hd + 1) * HG_DK)
            qt = hg_ref[rs, HG_QT + ls.start:HG_QT + ls.stop]
            v = hg_ref[rs, HG_V + ls.start:HG_V + ls.stop]
            st = states[hd]
            intra = _dot(amat_ref[rs, ls.start:ls.start + HG_CHUNK], v)
            oa_ref[rs, ls] = intra + _dot(qt, st.astype(BF16))
            decay = jnp.transpose(jnp.broadcast_to(dec_ref[c, :, ls], (HEAD_DV, HG_DK)))
            states[hd] = st * decay + _dot(kht_ref[c, ls, :], v)
    for hd in range(N_HEADS):
        s_ref[hd] = states[hd]


def _retention_recurrence(ret_ref, dm_ref, rdec_ref, r_ref, ob_ref):
    T = ret_ref.shape[0]
    first_head = lax.broadcasted_iota(jnp.int32, (RET_CHUNK, 2 * RET_DK), 1) < RET_DK
    bd_row = lax.broadcasted_iota(jnp.int32, (2 * RET_DK, 2 * HEAD_DV), 0)
    bd_col = lax.broadcasted_iota(jnp.int32, (2 * RET_DK, 2 * HEAD_DV), 1)
    bd_mask = (bd_row < RET_DK) == (bd_col < HEAD_DV)
    states = [r_ref[pr] for pr in range(N_PAIRS)]
    for c in range(T // RET_CHUNK):
        rs = slice(c * RET_CHUNK, (c + 1) * RET_CHUNK)
        for pr in range(N_PAIRS):
            qk = slice(pr * 2 * RET_DK, (pr + 1) * 2 * RET_DK)
            vs = slice(pr * 2 * HEAD_DV, (pr + 1) * 2 * HEAD_DV)
            rbd = states[pr]
            qr = ret_ref[rs, RT_Q + qk.start:RT_Q + qk.stop]
            kr = ret_ref[rs, RT_K + qk.start:RT_K + qk.stop]
            vpair = ret_ref[rs, RT_V + vs.start:RT_V + vs.stop]
            zero = jnp.zeros_like(qr)
            q0 = jnp.where(first_head, qr, zero)
            q1 = jnp.where(first_head, zero, qr)
            sc = lax.dot_general(jnp.concatenate([q0, q1], axis=0), kr, NT_DIMS, preferred_element_type=F32)
            sc0 = sc[:RET_CHUNK] * dm_ref[2 * pr]
            sc1 = sc[RET_CHUNK:] * dm_ref[2 * pr + 1]
            intra0 = _dot(sc0.astype(BF16), vpair[:, :HEAD_DV])
            intra1 = _dot(sc1.astype(BF16), vpair[:, HEAD_DV:])
            cross = _dot(ret_ref[rs, RT_QXI + qk.start:RT_QXI + qk.stop], rbd.astype(BF16))
            ob_ref[rs, vs] = jnp.concatenate([intra0, intra1], axis=1) + cross
            cstate = lax.dot_general(ret_ref[rs, RT_KZ + qk.start:RT_KZ + qk.stop], vpair, TN_DIMS,
                                     preferred_element_type=F32)
            states[pr] = rbd * rdec_ref[pr] + jnp.where(bd_mask, cstate, 0.0)
    for pr in range(N_PAIRS):
        r_ref[pr] = states[pr]


def _merge_out(x, gate, oa_ref, ob_ref, gate_ref, hgg_ref, rtg_ref, fg_ref, wout_ref, m_ref, o_ref):
    def normed(o, g):
        return o * lax.rsqrt(jnp.mean(o * o, axis=-1, keepdims=True) + EPS) * g

    for hd in range(N_HEADS):
        ls = slice(hd * HEAD_DV, (hd + 1) * HEAD_DV)
        ua = normed(oa_ref[:, ls], hgg_ref[:, ls]) * gate_ref[:, GT_A + ls.start:GT_A + ls.stop]
        ub = normed(ob_ref[:, ls], rtg_ref[:, ls]) * gate_ref[:, GT_B + ls.start:GT_B + ls.stop]
        m_ref[:, ls] = (ua + ub).astype(BF16)
    xo = x + gate * _dot(m_ref[...], wout_ref[...])
    o_ref[...] = xo * lax.rsqrt(jnp.mean(xo * xo, axis=-1, keepdims=True) + EPS) * fg_ref[...]


def _back_kernel(x_ref, mod_ref, hg_ref, kht_ref, ret_ref, gate_ref, amat_ref, dec_ref, hgg_ref, rtg_ref, fg_ref,
                 wout_ref, dm_ref, rdec_ref,
                 o_ref,
                 s_ref, r_ref, oa_ref, ob_ref, m_ref):
    @pl.when(pl.program_id(1) == 0)
    def _():
        s_ref[...] = jnp.zeros_like(s_ref)
        r_ref[...] = jnp.zeros_like(r_ref)

    _hgrn_recurrence(hg_ref, kht_ref, amat_ref, dec_ref, s_ref, oa_ref)
    _retention_recurrence(ret_ref, dm_ref, rdec_ref, r_ref, ob_ref)
    _merge_out(x_ref[...], mod_ref[:, 2 * D_MODEL:], oa_ref, ob_ref, gate_ref, hgg_ref, rtg_ref, fg_ref,
               wout_ref, m_ref, o_ref)


def _rotary_tables(seq_len):
    half = RET_DK // 2
    inv_freq = 1.0 / (ROPE_BASE ** jnp.linspace(0.0, 1.0, half, dtype=F32))
    ang = jnp.arange(seq_len, dtype=jnp.int32).astype(F32)[:, None] * inv_freq[None, :]
    cos, sin = jnp.cos(ang), jnp.sin(ang)
    zero = jnp.zeros_like(sin)
    cos_t = jnp.tile(cos, (1, 4))
    sina_t = jnp.tile(jnp.concatenate([-sin, zero], axis=1), (1, 2))
    sinb_t = jnp.tile(jnp.concatenate([zero, sin], axis=1), (1, 2))
    return cos_t, sina_t, sinb_t


def _retention_tables(tile_t):
    c = RET_CHUNK
    log_gamma = jnp.log(1.0 - jnp.exp2(-5.0 - jnp.arange(N_HEADS, dtype=F32)))
    idx = jnp.arange(c, dtype=F32)
    rel = idx[:, None] - idx[None, :]
    dm = jnp.where(rel >= 0, jnp.exp(log_gamma[:, None, None] * jnp.maximum(rel, 0.0)), 0.0)
    zeta = jnp.exp(log_gamma[:, None] * (c - 1.0 - idx))
    xi = jnp.exp(log_gamma[:, None] * (idx + 1.0))
    decay = jnp.exp(log_gamma * c)

    def per_pair_lanes(t):
        t = jnp.transpose(t.reshape(N_PAIRS, 2, c), (0, 2, 1))
        return jnp.tile(jnp.repeat(t, RET_DK, axis=2), (1, tile_t // c, 1))

    dec = jnp.repeat(decay.reshape(N_PAIRS, 1, 2), HEAD_DV, axis=2)
    return dm, per_pair_lanes(xi), per_pair_lanes(zeta), dec


def _chunk_tril(tile_t):
    r = jnp.arange(tile_t, dtype=jnp.int32)
    same_chunk = (r[:, None] // HG_CHUNK) == (r[None, :] // HG_CHUNK)
    return jnp.where(same_chunk & (r[:, None] >= r[None, :]), 1.0, 0.0).astype(BF16)


def _const_spec(shape, single_buffer=False):
    zeros = (0,) * len(shape)
    if single_buffer:
        return pl.BlockSpec(shape, lambda b, t: zeros, pipeline_mode=pl.Buffered(1))
    return pl.BlockSpec(shape, lambda b, t: zeros)


def _prep_call(c, w_ada, b_ada, lb_logits):
    bsz = c.shape[0]
    return pl.pallas_call(
        _prep_kernel,
        out_shape=(jax.ShapeDtypeStruct((bsz, 3 * D_MODEL), F32),
                   jax.ShapeDtypeStruct((1, N_HEADS * HG_DK), F32)),
        compiler_params=pltpu.CompilerParams(vmem_limit_bytes=V7X_VMEM_LIMIT_BYTES),
        name="adaln_prep",
    )(c, w_ada, b_ada.reshape(1, -1), lb_logits)


def _tile_spec(tile_t, width):
    return pl.BlockSpec((None, tile_t, width), lambda b, t: (b, t, 0))


def _front_call(x, mod3, norm_g, w_in, lb, tile_t):
    bsz, seq, d = x.shape
    n_hc = tile_t // HG_CHUNK
    cos_t, sina_t, sinb_t = _rotary_tables(seq)
    _, xi, zeta, _ = _retention_tables(tile_t)
    ltri = _chunk_tril(tile_t)
    rot_spec = pl.BlockSpec((tile_t, 2 * RET_DK), lambda b, t: (t, 0))
    dec_spec = pl.BlockSpec((None, n_hc, 1, d), lambda b, t: (b, t, 0, 0))
    kht_spec = pl.BlockSpec((None, n_hc, d, HG_CHUNK), lambda b, t: (b, t, 0, 0))
    bf = lambda width: jax.ShapeDtypeStruct((bsz, seq, width), BF16)
    tile_in_specs = [
        _tile_spec(tile_t, d),
        pl.BlockSpec((None, 1, 3 * d), lambda b, t: (b, 0, 0)),
        _const_spec((1, d)),
        _const_spec((1, d)),
        _const_spec(ltri.shape),
        rot_spec, rot_spec, rot_spec,
        _const_spec(xi.shape), _const_spec(zeta.shape),
    ]
    tile_out_specs = [
        _tile_spec(tile_t, HG_WIDTH), kht_spec, _tile_spec(tile_t, RT_WIDTH), _tile_spec(tile_t, GT_WIDTH),
        _tile_spec(tile_t, d), dec_spec,
    ]
    any_spec = pl.BlockSpec(memory_space=pl.ANY)
    return pl.pallas_call(
        functools.partial(_front_kernel, grid=(bsz, seq // tile_t),
                          in_specs=tile_in_specs, out_specs=tile_out_specs),
        in_specs=[any_spec] * 11,
        out_specs=[any_spec] * 6,
        out_shape=[bf(HG_WIDTH), jax.ShapeDtypeStruct((bsz, seq // HG_CHUNK, d, HG_CHUNK), BF16),
                   bf(RT_WIDTH), bf(GT_WIDTH), bf(d),
                   jax.ShapeDtypeStruct((bsz, seq // HG_CHUNK, 1, d), F32)],
        scratch_shapes=[
            pltpu.VMEM((tile_t, D_IN), F32),
            pltpu.VMEM((tile_t, d), BF16),
            pltpu.VMEM((tile_t, d), F32),
            pltpu.VMEM((tile_t, d), F32),
            pltpu.VMEM((tile_t, d), BF16),
            pltpu.VMEM((tile_t, d), BF16),
            pltpu.VMEM((d, D_IN), BF16),
            pltpu.VMEM((WEIGHT_STAGE_SLOTS, d, PROJ_BLOCK), F32),
            pltpu.SemaphoreType.DMA((WEIGHT_STAGE_SLOTS,)),
        ],
        compiler_params=pltpu.CompilerParams(vmem_limit_bytes=V7X_VMEM_LIMIT_BYTES),
        name="layer_front",
    )(x, mod3, norm_g.reshape(1, -1), w_in, lb, ltri, cos_t, sina_t, sinb_t, xi, zeta)


def _back_call(x, mod3, hg, kht, ret, gate, amat, dec, hg_g, ret_g, final_g, w_out, tile_t):
    bsz, seq, d = x.shape
    n_hc = tile_t // HG_CHUNK
    dm, _, _, rdec = _retention_tables(tile_t)
    row = lambda v: v.reshape(1, -1)
    return pl.pallas_call(
        _back_kernel,
        grid=(bsz, seq // tile_t),
        in_specs=[
            _tile_spec(tile_t, d),
            pl.BlockSpec((None, 1, 3 * d), lambda b, t: (b, 0, 0)),
            _tile_spec(tile_t, HG_WIDTH),
            pl.BlockSpec((None, n_hc, d, HG_CHUNK), lambda b, t: (b, t, 0, 0)),
            _tile_spec(tile_t, RT_WIDTH), _tile_spec(tile_t, GT_WIDTH),
            _tile_spec(tile_t, d),
            pl.BlockSpec((None, n_hc, 1, d), lambda b, t: (b, t, 0, 0)),
            _const_spec((1, d)), _const_spec((1, d)), _const_spec((1, d)),
            _const_spec((d, d), single_buffer=True),
            _const_spec(dm.shape), _const_spec(rdec.shape),
        ],
        out_specs=_tile_spec(tile_t, d),
        out_shape=jax.ShapeDtypeStruct(x.shape, x.dtype),
        scratch_shapes=[
            pltpu.VMEM((N_HEADS, HG_DK, HEAD_DV), F32),
            pltpu.VMEM((N_PAIRS, 2 * RET_DK, 2 * HEAD_DV), F32),
            pltpu.VMEM((tile_t, d), F32),
            pltpu.VMEM((tile_t, d), F32),
            pltpu.VMEM((tile_t, d), BF16),
        ],
        compiler_params=pltpu.CompilerParams(
            dimension_semantics=("arbitrary", "arbitrary"),
            vmem_limit_bytes=V7X_VMEM_LIMIT_BYTES),
        name="layer_back",
    )(x, mod3, hg, kht, ret, gate, amat, dec, row(hg_g), row(ret_g), row(final_g), w_out, dm, rdec)


def kernel(x, c, norm_g, w_ada, b_ada, w_in, hg_lb_logits, hg_norm_g, ret_norm_g, w_out, final_g):
    depth = norm_g.shape[0]
    assert depth == 1 and hg_lb_logits.shape[0] == 2
    bsz, seq, d = x.shape
    tile_t = min(TILE_T, seq)
    assert d == D_MODEL and seq % tile_t == 0 and tile_t % RET_CHUNK == 0
    mod, lb = _prep_call(c, w_ada[0], b_ada[0], hg_lb_logits)
    mod3 = mod.reshape(bsz, 1, 3 * d)
    hg, kht, ret, gate, amat, dec = _front_call(x, mod3, norm_g[0], w_in, lb, tile_t)
    return _back_call(x, mod3, hg, kht, ret, gate, amat, dec, hg_norm_g[0], ret_norm_g[0], final_g,
                      w_out[0].astype(BF16), min(TILE_BACK, seq))
```
